```python
import math
import jax
import jax.numpy as jnp
from jax import lax
import numpy as np

D_MODEL = 1024
BATCH = 2
SEQ = 8192
DEPTH = 2
DEC_BATCH = 128
DEC_SEQ = 8
PAST_LEN = 8192
PAGE_SIZE = 128

N_EVEN = (DEPTH + 1) // 2
N_ODD = DEPTH // 2
Q_BLOCK = 128
EPS = 1e-6
NEG = -1e30

H_A = 4
DH_A = 64
ROT_A = DH_A // 4
ROPE_THETA = 500000.0
SCALE_A = DH_A ** -0.5

H_B = 4
Q_LORA = 384
KV_LORA = 256
NOPE_B = 64
ROPE_B = 32
V_B = 128
MLA_THETA = 10000.0
SCALE_B = (NOPE_B + ROPE_B) ** -0.5

H_C = 16
KVH_C = 4
DH_C = 64
SCALE_C = DH_C ** -0.5

D_FF = 4 * D_MODEL

kernel_name = 'hybrid_diffattn_mla_stickbreak_step'


def rmsnorm(x, g):
    xf = x.astype(jnp.float32)
    y = xf * lax.rsqrt(jnp.mean(xf * xf, axis=-1, keepdims=True) + EPS)
    return (y * g.astype(jnp.float32)).astype(x.dtype)


def rope(x, pos, rot, theta):
    half = rot // 2
    inv = theta ** (-jnp.arange(half, dtype=jnp.float32) / half)
    ang = pos.astype(jnp.float32)[:, None] * inv[None, :]
    shp = (pos.shape[0],) + (1,) * (x.ndim - 3) + (half,)
    cos = jnp.cos(ang).reshape(shp).astype(x.dtype)
    sin = jnp.sin(ang).reshape(shp).astype(x.dtype)
    x1, x2, rest = x[..., :half], x[..., half:rot], x[..., rot:]
    return jnp.concatenate([x1 * cos - x2 * sin, x2 * cos + x1 * sin, rest], axis=-1)


def sq_relu_mlp(h, w1, w2):
    return jnp.square(jax.nn.relu(h @ w1)) @ w2


def gather_pages(cache, layer, page_table):
    g = cache[layer, page_table]
    return g.reshape((g.shape[0], g.shape[1] * g.shape[2]) + g.shape[3:])


def add_all(terms):
    out = terms[0]
    for t in terms[1:]:
        out = out + t
    return out


def even_project(h, pos, w_in, g_q, g_kv, w_uq, w_uk):
    b, t, _ = h.shape
    n_a = H_A * 2 * DH_A
    cuts = [n_a, 2 * n_a, 3 * n_a, 3 * n_a + Q_LORA, 3 * n_a + Q_LORA + KV_LORA]
    qd, kd, vd, cq, ckv, kr = jnp.split(h @ w_in, cuts, axis=-1)
    qd = rope(qd.reshape(b, t, H_A, 2, DH_A), pos, ROT_A, ROPE_THETA)
    kd = rope(kd.reshape(b, t, H_A, 2, DH_A), pos, ROT_A, ROPE_THETA)
    vd = vd.reshape(b, t, H_A, 2 * DH_A)
    q = (rmsnorm(cq, g_q) @ w_uq).reshape(b, t, H_B, NOPE_B + ROPE_B)
    q_lat = jnp.einsum('bthn,hcn->bthc', q[..., :NOPE_B], w_uk)
    q_rope = rope(q[..., NOPE_B:], pos, ROPE_B, MLA_THETA)
    ckv = rmsnorm(ckv, g_kv)
    kr = rope(kr, pos, ROPE_B, MLA_THETA)
    return (qd, q_lat, q_rope), (kd, vd, ckv, kr)


def even_core(qd, q_lat, q_rope, qpos, segs, lam):
    kpos = jnp.concatenate([s[4] for s in segs])
    mask = kpos[None, :] <= qpos[:, None]
    s_d = jnp.concatenate([jnp.einsum('bqhmd,bkhmd->bhmqk', qd, s[0]) for s in segs], axis=-1)
    p_d = jax.nn.softmax(jnp.where(mask, s_d.astype(jnp.float32) * SCALE_A, NEG), axis=-1)
    w_d = p_d[:, :, 0] - lam * p_d[:, :, 1]
    s_m = jnp.concatenate([jnp.einsum('bqhc,bkc->bhqk', q_lat, s[2]) + jnp.einsum('bqhr,bkr->bhqk', q_rope, s[3]) for s in segs], axis=-1)
    p_m = jax.nn.softmax(jnp.where(mask, s_m.astype(jnp.float32) * SCALE_B, NEG), axis=-1)
    o_d, o_l, off = [], [], 0
    for kd, vd, ckv, kr, kp in segs:
        n = kp.shape[0]
        o_d.append(jnp.einsum('bhqk,bkhe->bqhe', w_d[..., off:off + n].astype(vd.dtype), vd))
        o_l.append(jnp.einsum('bhqk,bkc->bqhc', p_m[..., off:off + n].astype(ckv.dtype), ckv))
        off += n
    return add_all(o_d), add_all(o_l)


def even_layer(h, pos, past, w_in, lam_vec, lam_init, g_sub, g_q, g_kv, w_uq, w_uk, w_uv, w_out):
    b, t, _ = h.shape
    (qd, ql, qr), (kd, vd, ckv, kr) = even_project(h, pos, w_in, g_q, g_kv, w_uq, w_uk)
    lf = lam_vec.astype(jnp.float32)
    lam = jnp.exp(jnp.sum(lf[0] * lf[1])) - jnp.exp(jnp.sum(lf[2] * lf[3])) + lam_init
    own = (kd, vd, ckv, kr, pos)
    if past is None:
        def blk(i):
            st = i * Q_BLOCK
            sl = lambda a: lax.dynamic_slice_in_dim(a, st, Q_BLOCK, axis=1)
            return even_core(sl(qd), sl(ql), sl(qr), st + jnp.arange(Q_BLOCK), [own], lam)
        od, ol = lax.map(blk, jnp.arange(t // Q_BLOCK))
        od = jnp.moveaxis(od, 0, 1).reshape(b, t, H_A, 2 * DH_A)
        ol = jnp.moveaxis(ol, 0, 1).reshape(b, t, H_B, KV_LORA)
    else:
        od, ol = even_core(qd, ql, qr, pos, [past, own], lam)
    od = rmsnorm(od, g_sub) * (1.0 - lam_init)
    om = jnp.einsum('bthc,hce->bthe', ol, w_uv)
    out = jnp.concatenate([od.reshape(b, t, -1), om.reshape(b, t, -1)], axis=-1) @ w_out
    return out, kd, vd, jnp.concatenate([ckv, kr], axis=-1)


def sb_core(q, qpos, segs):
    kpos = jnp.concatenate([s[2] for s in segs])
    mask = kpos[None, :] < qpos[:, None]
    z = jnp.concatenate([jnp.einsum('bqkgd,bskd->bkgqs', q, s[0]) for s in segs], axis=-1)
    z = z.astype(jnp.float32) * SCALE_C
    lneg = jnp.where(mask, jax.nn.log_sigmoid(-z), 0.0)
    log_a = jax.nn.log_sigmoid(z) + lax.cumsum(lneg, axis=4, reverse=True) - lneg
    a = jnp.where(mask, jnp.exp(log_a), 0.0)
    outs, off = [], 0
    for k, v, kp in segs:
        n = kp.shape[0]
        outs.append(jnp.einsum('bkgqs,bskd->bqkgd', a[..., off:off + n].astype(v.dtype), v))
        off += n
    return add_all(outs)


def odd_layer(h, pos, past, w_in, w_out):
    b, t, _ = h.shape
    nq = H_C * DH_C
    nk = KVH_C * DH_C
    q, k, v = jnp.split(h @ w_in, [nq, nq + nk], axis=-1)
    q = q.reshape(b, t, KVH_C, H_C // KVH_C, DH_C)
    k = k.reshape(b, t, KVH_C, DH_C)
    v = v.reshape(b, t, KVH_C, DH_C)
    own = (k, v, pos)
    if past is None:
        def blk(i):
            st = i * Q_BLOCK
            qb = lax.dynamic_slice_in_dim(q, st, Q_BLOCK, axis=1)
            return sb_core(qb, st + jnp.arange(Q_BLOCK), [own])
        o = lax.map(blk, jnp.arange(t // Q_BLOCK))
        o = jnp.moveaxis(o, 0, 1).reshape(b, t, nq)
    else:
        o = sb_core(q, pos, [past, own]).reshape(b, t, nq)
    return o @ w_out, k, v


def setup_inputs(seed: int = 0) -> dict:
    key = jax.random.key(seed)
    ks = iter(jax.random.split(key, 32))

    def nrm(shape, scale=1.0):
        return jax.random.normal(next(ks), shape, jnp.float32) * scale

    def gain(shape):
        return 1.0 + nrm(shape, 0.02)

    n_pages = PAST_LEN // PAGE_SIZE
    n_used = DEC_BATCH * n_pages
    n_pool = n_used + n_used // 4
    n_a = H_A * 2 * DH_A
    d_in_even = 3 * n_a + Q_LORA + KV_LORA + ROPE_B
    d_out_even = n_a + H_B * V_B
    d_in_odd = H_C * DH_C + 2 * KVH_C * DH_C
    perm = jax.random.permutation(next(ks), n_pool).astype(jnp.int32)
    page_table = perm[:n_used].reshape(DEC_BATCH, n_pages)
    return {
        'x_prompt': nrm((BATCH, SEQ, D_MODEL)),
        'x_sample': nrm((DEC_BATCH, DEC_SEQ, D_MODEL)),
        'cache_diff_k': nrm((N_EVEN, n_pool, PAGE_SIZE, H_A, 2, DH_A)),
        'cache_diff_v': nrm((N_EVEN, n_pool, PAGE_SIZE, H_A, 2 * DH_A)),
        'cache_mla': nrm((N_EVEN, n_pool, PAGE_SIZE, KV_LORA + ROPE_B)),
        'cache_sb_k': nrm((N_ODD, n_pool, PAGE_SIZE, KVH_C, DH_C)),
        'cache_sb_v': nrm((N_ODD, n_pool, PAGE_SIZE, KVH_C, DH_C)),
        'page_table': page_table,
        'g_mix': gain((DEPTH, D_MODEL)),
        'g_ffn': gain((DEPTH, D_MODEL)),
        'w_in_even': nrm((N_EVEN, D_MODEL, d_in_even), D_MODEL ** -0.5),
        'diff_lambda': nrm((N_EVEN, 4, DH_A), 0.1),
        'g_diff_sub': gain((N_EVEN, 2 * DH_A)),
        'g_mla_q': gain((N_EVEN, Q_LORA)),
        'g_mla_kv': gain((N_EVEN, KV_LORA)),
        'w_mla_uq': nrm((N_EVEN, Q_LORA, H_B * (NOPE_B + ROPE_B)), Q_LORA ** -0.5),
        'w_mla_uk': nrm((N_EVEN, H_B, KV_LORA, NOPE_B), KV_LORA ** -0.5),
        'w_mla_uv': nrm((N_EVEN, H_B, KV_LORA, V_B), KV_LORA ** -0.5),
        'w_out_even': nrm((N_EVEN, d_out_even, D_MODEL), d_out_even ** -0.5),
        'w_in_odd': nrm((N_ODD, D_MODEL, d_in_odd), D_MODEL ** -0.5),
        'w_out_odd': nrm((N_ODD, H_C * DH_C, D_MODEL), (H_C * DH_C) ** -0.5),
        'w_ff1': nrm((DEPTH, D_MODEL, D_FF), D_MODEL ** -0.5),
        'w_ff2': nrm((DEPTH, D_FF, D_MODEL), D_FF ** -0.5),
        'g_final': gain((D_MODEL,)),
    }


def reference(x_prompt, x_sample, cache_diff_k, cache_diff_v, cache_mla, cache_sb_k, cache_sb_v, page_table,
              g_mix, g_ffn, w_in_even, diff_lambda, g_diff_sub, g_mla_q, g_mla_kv, w_mla_uq, w_mla_uk, w_mla_uv,
              w_out_even, w_in_odd, w_out_odd, w_ff1, w_ff2, g_final):
    s_len = x_prompt.shape[1]
    t_len = x_sample.shape[1]
    past_len = page_table.shape[1] * PAGE_SIZE
    pos_p = jnp.arange(s_len)
    pos_past = jnp.arange(past_len)
    pos_s = past_len + jnp.arange(t_len)
    xp, xs = x_prompt, x_sample
    dk_p, dv_p, ml_p, sk_p, sv_p = [], [], [], [], []
    dk_s, dv_s, ml_s, sk_s, sv_s = [], [], [], [], []
    for l in range(DEPTH):
        hp = rmsnorm(xp, g_mix[l])
        hs = rmsnorm(xs, g_mix[l])
        if l % 2 == 0:
            e = l // 2
            lam_init = 0.8 - 0.6 * math.exp(-0.3 * l)
            prm = (w_in_even[e], diff_lambda[e], lam_init, g_diff_sub[e], g_mla_q[e], g_mla_kv[e],
                   w_mla_uq[e], w_mla_uk[e], w_mla_uv[e], w_out_even[e])
            mix_p, kd, vd, ml = even_layer(hp, pos_p, None, *prm)
            dk_p.append(kd)
            dv_p.append(vd)
            ml_p.append(ml)
            mla_past = gather_pages(cache_mla, e, page_table)
            past = (gather_pages(cache_diff_k, e, page_table), gather_pages(cache_diff_v, e, page_table),
                    mla_past[..., :KV_LORA], mla_past[..., KV_LORA:], pos_past)
            mix_s, kd, vd, ml = even_layer(hs, pos_s, past, *prm)
            dk_s.append(kd)
            dv_s.append(vd)
            ml_s.append(ml)
        else:
            o = l // 2
            mix_p, k, v = odd_layer(hp, pos_p, None, w_in_odd[o], w_out_odd[o])
            sk_p.append(k)
            sv_p.append(v)
            past = (gather_pages(cache_sb_k, o, page_table), gather_pages(cache_sb_v, o, page_table), pos_past)
            mix_s, k, v = odd_layer(hs, pos_s, past, w_in_odd[o], w_out_odd[o])
            sk_s.append(k)
            sv_s.append(v)
        xp = xp + mix_p
        xs = xs + mix_s
        xp = xp + sq_relu_mlp(rmsnorm(xp, g_ffn[l]), w_ff1[l], w_ff2[l])
        xs = xs + sq_relu_mlp(rmsnorm(xs, g_ffn[l]), w_ff1[l], w_ff2[l])
    y_prompt = rmsnorm(xp, g_final)
    y_sample = rmsnorm(xs, g_final)
    return (y_prompt, y_sample,
            jnp.stack(dk_p), jnp.stack(dv_p), jnp.stack(ml_p), jnp.stack(sk_p), jnp.stack(sv_p),
            jnp.stack(dk_s), jnp.stack(dv_s), jnp.stack(ml_s), jnp.stack(sk_s), jnp.stack(sv_s))
```

```python
import functools
import math

import jax
import jax.numpy as jnp
from jax import lax
from jax.experimental import pallas as pl
from jax.experimental.pallas import tpu as pltpu

F32 = jnp.float32
BF16 = jnp.bfloat16

EPS = 1e-6
NEG = -1e30
H_A = 4
DH_A = 64
ROT_A = DH_A // 4
ROPE_THETA = 500000.0
SCALE_A = DH_A ** -0.5
H_B = 4
Q_LORA = 384
KV_LORA = 256
NOPE_B = 64
ROPE_B = 32
V_B = 128
MLA_THETA = 10000.0
SCALE_B = (NOPE_B + ROPE_B) ** -0.5
H_C = 16
KVH_C = 4
G_C = H_C // KVH_C
DH_C = 64
SCALE_C = DH_C ** -0.5
PAGE = 128

LANES = 128
VMEM_LIMIT = 48 * 1024 * 1024

N_A = H_A * 2 * DH_A
MLA_W = KV_LORA + LANES


def _tile(n, pref):
    t = min(n, pref)
    while n % t:
        t //= 2
    return t


def _dot(a, b):
    return jnp.dot(a, b, preferred_element_type=F32)


def _dot_nt(a, b):
    return lax.dot_general(a, b, (((1,), (1,)), ((), ())), preferred_element_type=F32)


def _rep(x, width):
    n = width // LANES
    return x if n == 1 else pltpu.repeat(x, n, 1)


def _rms(x, g):
    return x * lax.rsqrt(jnp.mean(x * x, axis=-1, keepdims=True) + EPS) * g


def _cparams(sem, vmem=VMEM_LIMIT):
    return pltpu.CompilerParams(dimension_semantics=sem, vmem_limit_bytes=vmem)


def _rope_tables(pos):
    posf = pos.astype(F32)[:, None]
    lane = jnp.arange(LANES)

    def tab(period, half, theta):
        inv = theta ** (-jnp.arange(half, dtype=F32) / half)
        ang = posf * inv[None, :]
        cos, sin = jnp.cos(ang), jnp.sin(ang)
        d = lane % period
        first = d < half
        second = (d >= half) & (d < 2 * half)
        idx = jnp.where(first, d, jnp.where(second, d - half, 0))
        c = jnp.where((first | second)[None, :], cos[:, idx], 1.0)
        s1 = jnp.where(first[None, :], -sin[:, idx], 0.0)
        s2 = jnp.where(second[None, :], sin[:, idx], 0.0)
        return [c, s1, s2]

    return jnp.concatenate(tab(DH_A, ROT_A // 2, ROPE_THETA) + tab(ROPE_B, ROPE_B // 2, MLA_THETA), axis=1)


def _rope(y, c, s1, s2, half):
    outs = []
    for k in range(y.shape[1] // LANES):
        yb = y[:, k * LANES:(k + 1) * LANES]
        outs.append(yb * c + pltpu.roll(yb, LANES - half, 1) * s1 + pltpu.roll(yb, half, 1) * s2)
    return outs[0] if len(outs) == 1 else jnp.concatenate(outs, axis=1)


def _even_proj_kernel(x_ref, g_ref, w_ref, tab_ref, gq_ref, gkv_ref, wuq_ref, wuk_ref,
                      kd_o, vd_o, mla_o, qd_b, kd_b, vd_b, qm_b, km_b):
    hb = _rms(x_ref[...], g_ref[...]).astype(BF16)
    tab = tab_ref[...]
    ca, s1a, s2a, cb, s1b, s2b = [tab[:, k * LANES:(k + 1) * LANES] for k in range(6)]
    o = 0
    qd = _rope(_dot(hb, w_ref[:, o:o + N_A]), ca, s1a, s2a, ROT_A // 2)
    qd_b[...] = (qd * SCALE_A).astype(BF16)
    o += N_A
    kd = _rope(_dot(hb, w_ref[:, o:o + N_A]), ca, s1a, s2a, ROT_A // 2)
    kd_o[...] = kd
    kd_b[...] = kd.astype(BF16)
    o += N_A
    vd = _dot(hb, w_ref[:, o:o + N_A])
    vd_o[...] = vd
    vd_b[...] = vd.astype(BF16)
    o += N_A
    cq = _dot(hb, w_ref[:, o:o + Q_LORA])
    o += Q_LORA
    ckv = _dot(hb, w_ref[:, o:o + KV_LORA])
    o += KV_LORA
    kr4 = _dot(hb, w_ref[:, o:o + LANES])
    cqn = _rms(cq, gq_ref[...]).astype(BF16)
    qn = _dot(cqn, wuq_ref[:, 0:H_B * NOPE_B]).astype(BF16)
    qr = _dot(cqn, wuq_ref[:, H_B * NOPE_B:])
    qlat = _dot(qn, wuk_ref[...])
    qr = _rope(qr, cb, s1b, s2b, ROPE_B // 2)
    ckvn = _rms(ckv, gkv_ref[...])
    kr4 = _rope(kr4, cb, s1b, s2b, ROPE_B // 2)
    mla_o[:, 0:KV_LORA] = ckvn
    mla_o[:, KV_LORA:KV_LORA + ROPE_B] = kr4[:, 0:ROPE_B]
    km_b[:, 0:KV_LORA] = ckvn.astype(BF16)
    km_b[:, KV_LORA:] = kr4.astype(BF16)
    lane = lax.broadcasted_iota(jnp.int32, (1, LANES), 1)
    for h in range(H_B):
        qm_b[h, :, 0:KV_LORA] = (qlat[:, h * KV_LORA:(h + 1) * KV_LORA] * SCALE_B).astype(BF16)
        qm_b[h, :, KV_LORA:] = jnp.where((lane >= h * ROPE_B) & (lane < (h + 1) * ROPE_B), qr * SCALE_B, 0.0).astype(BF16)


def _even_proj(x, g, w, tab, gq, gkv, wuq, wuk, tm):
    r, d = x.shape
    nt = tab.shape[0] // tm
    wcols = w.shape[1]
    row = lambda n: pl.BlockSpec((tm, n), lambda i: (i, 0))
    full = lambda a: pl.BlockSpec(a.shape, lambda i: (0,) * a.ndim)
    out_shape = (
        jax.ShapeDtypeStruct((r, N_A), F32), jax.ShapeDtypeStruct((r, N_A), F32),
        jax.ShapeDtypeStruct((r, KV_LORA + ROPE_B), F32),
        jax.ShapeDtypeStruct((r, N_A), BF16), jax.ShapeDtypeStruct((r, N_A), BF16),
        jax.ShapeDtypeStruct((r, N_A), BF16),
        jax.ShapeDtypeStruct((H_B, r, MLA_W), BF16), jax.ShapeDtypeStruct((r, MLA_W), BF16))
    out_specs = (row(N_A), row(N_A), row(KV_LORA + ROPE_B), row(N_A), row(N_A), row(N_A),
                 pl.BlockSpec((H_B, tm, MLA_W), lambda i: (0, i, 0)), row(MLA_W))
    return pl.pallas_call(
        _even_proj_kernel, grid=(r // tm,),
        in_specs=[row(d), full(g), full(w), pl.BlockSpec((tm, tab.shape[1]), lambda i: (i % nt, 0)),
                  full(gq), full(gkv), full(wuq), full(wuk)],
        out_specs=out_specs, out_shape=out_shape,
        compiler_params=_cparams(("parallel",)), name="even_proj",
    )(x, g, w, tab, gq, gkv, wuq, wuk)


def _lam(lam_ref, lam_init):
    lf = lam_ref[...]
    a = jnp.sum(lf[0:1] * lf[1:2], axis=-1, keepdims=True)
    b = jnp.sum(lf[2:3] * lf[3:4], axis=-1, keepdims=True)
    return jnp.exp(a) - jnp.exp(b) + lam_init


def _softmax_step(s, vb, m_ref, l_ref, acc_ref):
    m_prev = m_ref[...]
    m_new = jnp.maximum(m_prev, jnp.max(s, axis=-1, keepdims=True))
    alpha = jnp.exp(m_prev - m_new)
    p = jnp.exp(s - _rep(m_new, s.shape[1]))
    l_ref[...] = alpha * l_ref[...] + jnp.sum(p, axis=-1, keepdims=True)
    pv = _dot(p.astype(BF16), vb)
    acc_ref[...] = _rep(alpha, acc_ref.shape[1]) * acc_ref[...] + pv
    m_ref[...] = m_new


def _diff_attn_kernel(q_ref, k_ref, v_ref, lam_ref, gsub_ref, o_ref, qq_ref, m_ref, l_ref, acc_ref,
                      *, tq, tk, lam_init):
    i = pl.program_id(2)
    lane = lax.broadcasted_iota(jnp.int32, (1, LANES), 1)
    q = q_ref[...]
    zero = jnp.zeros_like(q)
    qq_ref[0:tq, :] = jnp.where(lane < DH_A, q, zero)
    qq_ref[tq:2 * tq, :] = jnp.where(lane >= DH_A, q, zero)
    m_ref[...] = jnp.full(m_ref.shape, NEG, F32)
    l_ref[...] = jnp.zeros(l_ref.shape, F32)
    acc_ref[...] = jnp.zeros(acc_ref.shape, F32)

    def block(off, mask):
        s = _dot_nt(qq_ref[...], k_ref[pl.ds(off, tk), :])
        if mask is not None:
            s = jnp.where(mask, s, NEG)
        _softmax_step(s, v_ref[pl.ds(off, tk), :], m_ref, l_ref, acc_ref)

    def body(j, carry):
        block(pl.multiple_of(j * tk, tk), None)
        return carry

    lax.fori_loop(0, i * (tq // tk), body, 0)
    rows = lax.broadcasted_iota(jnp.int32, (2 * tq, 1), 0) % tq
    cols = lax.broadcasted_iota(jnp.int32, (1, tk), 1)
    for d in range(tq // tk):
        block(pl.multiple_of(i * tq + d * tk, tk), (cols + d * tk) <= rows)

    o = acc_ref[...] / l_ref[...]
    od = o[0:tq] - _lam(lam_ref, lam_init) * o[tq:2 * tq]
    o_ref[...] = (_rms(od, gsub_ref[...]) * (1.0 - lam_init)).astype(o_ref.dtype)


def _diff_attn(qd, kd, vd, lam_vec, gsub, b, t, lam_init):
    tq = _tile(t, 512)
    tk = _tile(tq, 512)
    nq = t // tq
    return pl.pallas_call(
        functools.partial(_diff_attn_kernel, tq=tq, tk=tk, lam_init=lam_init),
        grid=(b, H_A, nq),
        in_specs=[pl.BlockSpec((tq, LANES), lambda bb, h, i: (bb * nq + i, h)),
                  pl.BlockSpec((t, LANES), lambda bb, h, i: (bb, h)),
                  pl.BlockSpec((t, LANES), lambda bb, h, i: (bb, h)),
                  pl.BlockSpec(lam_vec.shape, lambda bb, h, i: (0, 0)),
                  pl.BlockSpec(gsub.shape, lambda bb, h, i: (0, 0))],
        out_specs=pl.BlockSpec((tq, LANES), lambda bb, h, i: (bb * nq + i, h)),
        out_shape=jax.ShapeDtypeStruct((b * t, N_A), BF16),
        scratch_shapes=[pltpu.VMEM((2 * tq, LANES), BF16), pltpu.VMEM((2 * tq, LANES), F32),
                        pltpu.VMEM((2 * tq, LANES), F32), pltpu.VMEM((2 * tq, LANES), F32)],
        compiler_params=_cparams(("parallel", "parallel", "arbitrary")), name="diff_attn",
    )(qd, kd, vd, lam_vec, gsub)


def _mla_attn_kernel(q_ref, k_ref, wuv_ref, o_ref, qq_ref, m_ref, l_ref, acc_ref, *, tq, tk):
    i = pl.program_id(1)
    for h in range(H_B):
        qq_ref[h * tq:(h + 1) * tq, :] = q_ref[h]
    m_ref[...] = jnp.full(m_ref.shape, NEG, F32)
    l_ref[...] = jnp.zeros(l_ref.shape, F32)
    acc_ref[...] = jnp.zeros(acc_ref.shape, F32)

    def block(off, mask):
        kb = k_ref[pl.ds(off, tk), :]
        s = _dot_nt(qq_ref[...], kb)
        if mask is not None:
            s = jnp.where(mask, s, NEG)
        _softmax_step(s, kb[:, 0:KV_LORA], m_ref, l_ref, acc_ref)

    def body(j, carry):
        block(pl.multiple_of(j * tk, tk), None)
        return carry

    lax.fori_loop(0, i * (tq // tk), body, 0)
    rows = lax.broadcasted_iota(jnp.int32, (H_B * tq, 1), 0) % tq
    cols = lax.broadcasted_iota(jnp.int32, (1, tk), 1)
    for d in range(tq // tk):
        block(pl.multiple_of(i * tq + d * tk, tk), (cols + d * tk) <= rows)

    ol = (acc_ref[...] / _rep(l_ref[...], KV_LORA)).astype(BF16)
    for h in range(H_B):
        o_ref[:, h * V_B:(h + 1) * V_B] = _dot(ol[h * tq:(h + 1) * tq], wuv_ref[h]).astype(o_ref.dtype)


def _mla_attn(qm, km, wuv, b, t):
    tq = _tile(t, 256)
    tk = _tile(tq, 256)
    nq = t // tq
    return pl.pallas_call(
        functools.partial(_mla_attn_kernel, tq=tq, tk=tk),
        grid=(b, nq),
        in_specs=[pl.BlockSpec((H_B, tq, MLA_W), lambda bb, i: (0, bb * nq + i, 0)),
                  pl.BlockSpec((t, MLA_W), lambda bb, i: (bb, 0)),
                  pl.BlockSpec(wuv.shape, lambda bb, i: (0, 0, 0))],
        out_specs=pl.BlockSpec((tq, H_B * V_B), lambda bb, i: (bb * nq + i, 0)),
        out_shape=jax.ShapeDtypeStruct((b * t, H_B * V_B), BF16),
        scratch_shapes=[pltpu.VMEM((H_B * tq, MLA_W), BF16), pltpu.VMEM((H_B * tq, LANES), F32),
                        pltpu.VMEM((H_B * tq, LANES), F32), pltpu.VMEM((H_B * tq, KV_LORA), F32)],
        compiler_params=_cparams(("parallel", "arbitrary")), name="mla_attn",
    )(qm, km, wuv)


def _post_kernel(x_ref, a1_ref, a2_ref, wo_ref, g_ref, w1_ref, w2_ref, gf_ref, o_ref,
                 x1_ref, hn_ref, acc_ref, *, final):
    j = pl.program_id(1)
    half = a1_ref.shape[1]

    @pl.when(j == 0)
    def _():
        mix = (_dot(a1_ref[...].astype(BF16), wo_ref[0:half, :])
               + _dot(a2_ref[...].astype(BF16), wo_ref[half:2 * half, :]))
        x1 = x_ref[...] + mix
        x1_ref[...] = x1
        hn_ref[...] = _rms(x1, g_ref[...]).astype(BF16)
        acc_ref[...] = jnp.zeros(acc_ref.shape, F32)

    h1 = jnp.maximum(_dot(hn_ref[...], w1_ref[...]), 0.0)
    acc_ref[...] += _dot((h1 * h1).astype(BF16), w2_ref[...])

    @pl.when(j == pl.num_programs(1) - 1)
    def _():
        y = x1_ref[...] + acc_ref[...]
        if final:
            y = _rms(y, gf_ref[...])
        o_ref[...] = y


def _post(x, a1, c1, a2, c2, wo, g, w1, w2, gf, final):
    r, d = x.shape
    tm = _tile(r, 512)
    dff = w1.shape[1]
    tf = _tile(dff, 1024)
    half = d // 2
    return pl.pallas_call(
        functools.partial(_post_kernel, final=final),
        grid=(r // tm, dff // tf),
        in_specs=[pl.BlockSpec((tm, d), lambda i, j: (i, 0)),
                  pl.BlockSpec((tm, half), lambda i, j: (i, c1)),
                  pl.BlockSpec((tm, half), lambda i, j: (i, c2)),
                  pl.BlockSpec(wo.shape, lambda i, j: (0, 0)),
                  pl.BlockSpec(g.shape, lambda i, j: (0, 0)),
                  pl.BlockSpec((d, tf), lambda i, j: (0, j)),
                  pl.BlockSpec((tf, d), lambda i, j: (j, 0)),
                  pl.BlockSpec(gf.shape, lambda i, j: (0, 0))],
        out_specs=pl.BlockSpec((tm, d), lambda i, j: (i, 0)),
        out_shape=jax.ShapeDtypeStruct((r, d), F32),
        scratch_shapes=[pltpu.VMEM((tm, d), F32), pltpu.VMEM((tm, d), BF16), pltpu.VMEM((tm, d), F32)],
        compiler_params=_cparams(("parallel", "arbitrary")), name="post_final" if final else "post",
    )(x, a1, a2, wo, g, w1, w2, gf)


NQ_C = H_C * DH_C
NK_C = KVH_C * DH_C


def _odd_proj_kernel(x_ref, g_ref, w_ref, k_o, v_o, q_b, k2_b, v2_b):
    hb = _rms(x_ref[...], g_ref[...]).astype(BF16)
    o = 0
    q_b[...] = (_dot(hb, w_ref[:, o:o + NQ_C]) * SCALE_C).astype(BF16)
    o += NQ_C
    k_o[...] = _dot(hb, w_ref[:, o:o + NK_C])
    o += NK_C
    v_o[...] = _dot(hb, w_ref[:, o:o + NK_C])
    o += NK_C
    k2_b[...] = _dot(hb, w_ref[:, o:o + 2 * NK_C]).astype(BF16)
    o += 2 * NK_C
    v2_b[...] = _dot(hb, w_ref[:, o:o + 2 * NK_C]).astype(BF16)


def _odd_proj(x, g, w, tm):
    r, d = x.shape
    row = lambda n: pl.BlockSpec((tm, n), lambda i: (i, 0))
    full = lambda a: pl.BlockSpec(a.shape, lambda i: (0,) * a.ndim)
    return pl.pallas_call(
        _odd_proj_kernel, grid=(r // tm,),
        in_specs=[row(d), full(g), full(w)],
        out_specs=(row(NK_C), row(NK_C), row(NQ_C), row(2 * NK_C), row(2 * NK_C)),
        out_shape=(jax.ShapeDtypeStruct((r, NK_C), F32), jax.ShapeDtypeStruct((r, NK_C), F32),
                   jax.ShapeDtypeStruct((r, NQ_C), BF16), jax.ShapeDtypeStruct((r, 2 * NK_C), BF16),
                   jax.ShapeDtypeStruct((r, 2 * NK_C), BF16)),
        compiler_params=_cparams(("parallel",)), name="odd_proj",
    )(x, g, w)


def _sb_step(z, vb, u, mask, c_ref, acc_ref):
    tk = z.shape[1]
    ls = jnp.minimum(z, 0.0) - jnp.log(1.0 + jnp.exp(-jnp.abs(z)))
    lneg = ls - z
    if mask is not None:
        lneg = jnp.where(mask, lneg, 0.0)
    hi = lneg.astype(BF16)
    lo = (lneg - hi.astype(F32)).astype(BF16)
    suf = _dot(hi, u) + _dot(lo, u)
    c = c_ref[...]
    a = jnp.exp(ls + suf + _rep(c, tk))
    if mask is not None:
        a = jnp.where(mask, a, 0.0)
    acc_ref[...] += _dot(a.astype(BF16), vb)
    c_ref[...] = c + (suf[:, 0:1] + lneg[:, 0:1])


def _sb_attn_kernel(q_ref, k_ref, v_ref, u_ref, o_ref, qq_ref, c_ref, acc_ref, *, tq, tk):
    i = pl.program_id(2)
    lane = lax.broadcasted_iota(jnp.int32, (1, LANES), 1)
    for g in range(G_C):
        qb = q_ref[:, (g // 2) * LANES:(g // 2 + 1) * LANES]
        keep = (lane < DH_C) if g % 2 == 0 else (lane >= DH_C)
        qq_ref[g * tq:(g + 1) * tq, :] = jnp.where(keep, qb, jnp.zeros_like(qb))
    c_ref[...] = jnp.zeros(c_ref.shape, F32)
    acc_ref[...] = jnp.zeros(acc_ref.shape, F32)

    def block(off, mask):
        z = _dot_nt(qq_ref[...], k_ref[pl.ds(off, tk), :])
        _sb_step(z, v_ref[pl.ds(off, tk), :], u_ref[...], mask, c_ref, acc_ref)

    rows = lax.broadcasted_iota(jnp.int32, (G_C * tq, 1), 0) % tq
    cols = lax.broadcasted_iota(jnp.int32, (1, tk), 1)
    for d in reversed(range(tq // tk)):
        block(pl.multiple_of(i * tq + d * tk, tk), (cols + d * tk) < rows)
    nfull = i * (tq // tk)

    def body(j, carry):
        block(pl.multiple_of((nfull - 1 - j) * tk, tk), None)
        return carry

    lax.fori_loop(0, nfull, body, 0)
    acc = acc_ref[...]
    for c2 in range(G_C // 2):
        o_ref[:, c2 * LANES:(c2 + 1) * LANES] = jnp.where(
            lane < DH_C, acc[(2 * c2) * tq:(2 * c2 + 1) * tq], acc[(2 * c2 + 1) * tq:(2 * c2 + 2) * tq]
        ).astype(o_ref.dtype)


def _tri(tk):
    j = jnp.arange(tk)
    return (j[:, None] > j[None, :]).astype(BF16)


def _sb_attn(q, k2, v2, b, t):
    tq = _tile(t, 256)
    tk = _tile(tq, 256)
    nq = t // tq
    u = _tri(tk)
    return pl.pallas_call(
        functools.partial(_sb_attn_kernel, tq=tq, tk=tk),
        grid=(b, KVH_C, nq),
        in_specs=[pl.BlockSpec((tq, G_C * DH_C), lambda bb, h, i: (bb * nq + i, h)),
                  pl.BlockSpec((t, LANES), lambda bb, h, i: (bb, h)),
                  pl.BlockSpec((t, LANES), lambda bb, h, i: (bb, h)),
                  pl.BlockSpec(u.shape, lambda bb, h, i: (0, 0))],
        out_specs=pl.BlockSpec((tq, G_C * DH_C), lambda bb, h, i: (bb * nq + i, h)),
        out_shape=jax.ShapeDtypeStruct((b * t, NQ_C), BF16),
        scratch_shapes=[pltpu.VMEM((G_C * tq, LANES), BF16), pltpu.VMEM((G_C * tq, LANES), F32),
                        pltpu.VMEM((G_C * tq, LANES), F32)],
        compiler_params=_cparams(("parallel", "parallel", "arbitrary")), name="sb_attn",
    )(q, k2, v2, u)


def _dec_even_kernel(pt_ref, qd_ref, ql_ref, qr_ref, ok_ref, ov_ref, om_ref, lam_ref, gsub_ref, *rest,
                     npg, ts, lam_init):
    kp = rest[0:npg]
    vp = rest[npg:2 * npg]
    mp = rest[2 * npg:3 * npg]
    od_ref, ol_ref = rest[3 * npg:3 * npg + 2]
    md_ref, ld_ref, accd_ref, mm_ref, lm_ref, accm_ref = rest[3 * npg + 2:]
    s_id = pl.program_id(1)

    @pl.when(s_id == 0)
    def _():
        md_ref[...] = jnp.full(md_ref.shape, NEG, F32)
        ld_ref[...] = jnp.zeros(ld_ref.shape, F32)
        accd_ref[...] = jnp.zeros(accd_ref.shape, F32)
        mm_ref[...] = jnp.full(mm_ref.shape, NEG, F32)
        lm_ref[...] = jnp.zeros(lm_ref.shape, F32)
        accm_ref[...] = jnp.zeros(accm_ref.shape, F32)

    qd = qd_ref[...]
    ql = ql_ref[...]
    qr = qr_ref[...]

    def block(kb, vb, mb, dmask, mmask):
        sd = _dot_nt(qd, kb.astype(BF16))
        if dmask is not None:
            sd = jnp.where(dmask, sd, NEG)
        _softmax_step(sd, vb.astype(BF16), md_ref, ld_ref, accd_ref)
        mbb = mb.astype(BF16)
        sm = _dot_nt(ql, mbb[:, 0:KV_LORA]) + _dot_nt(qr, mbb[:, KV_LORA:KV_LORA + ROPE_B])
        if mmask is not None:
            sm = jnp.where(mmask, sm, NEG)
        _softmax_step(sm, mbb[:, 0:KV_LORA], mm_ref, lm_ref, accm_ref)

    for p in range(npg):
        block(kp[p][...], vp[p][...], mp[p][...], None, None)

    @pl.when(s_id == pl.num_programs(1) - 1)
    def _():
        key = lax.broadcasted_iota(jnp.int32, (1, PAGE), 1)
        tok_d = lax.broadcasted_iota(jnp.int32, (H_A * 2 * ts, 1), 0) % ts
        tok_m = lax.broadcasted_iota(jnp.int32, (H_B * ts, 1), 0) % ts
        block(ok_ref[...], ov_ref[...], om_ref[...], key <= tok_d, key <= tok_m)
        o = accd_ref[...] / _rep(ld_ref[...], accd_ref.shape[1])
        lam = _lam(lam_ref, lam_init)
        for h in range(H_A):
            blk = o[h * 2 * ts:(h + 1) * 2 * ts, h * LANES:(h + 1) * LANES]
            od = blk[0:ts] - lam * blk[ts:2 * ts]
            od_ref[:, h * LANES:(h + 1) * LANES] = (_rms(od, gsub_ref[...]) * (1.0 - lam_init)).astype(od_ref.dtype)
        ol_ref[...] = accm_ref[...] / _rep(lm_ref[...], KV_LORA)


def _dec_even(pt, qd, ql, qr, own_k, own_v, own_m, lam_vec, gsub, ck, cv, cm, layer, npg, lam_init):
    nb, n_pages = pt.shape
    ts = qd.shape[1] // (H_A * 2)
    nsteps = n_pages // npg
    ptf = pt.reshape(-1)
    wk, wv, wm = ck.shape[-1], cv.shape[-1], cm.shape[-1]

    def page_spec(w, p):
        return pl.BlockSpec((None, None, PAGE, w),
                            lambda b, s, pt_ref: (layer, pt_ref[b * n_pages + s * npg + p], 0, 0))

    per_b = lambda a: pl.BlockSpec((None,) + a.shape[1:], lambda b, s, pt_ref: (b,) + (0,) * (a.ndim - 1))
    full = lambda a: pl.BlockSpec(a.shape, lambda b, s, pt_ref: (0,) * a.ndim)
    in_specs = ([per_b(qd), per_b(ql), per_b(qr), per_b(own_k), per_b(own_v), per_b(own_m),
                 full(lam_vec), full(gsub)]
                + [page_spec(wk, p) for p in range(npg)]
                + [page_spec(wv, p) for p in range(npg)]
                + [page_spec(wm, p) for p in range(npg)])
    rd, rm = qd.shape[1], ql.shape[1]
    grid_spec = pltpu.PrefetchScalarGridSpec(
        num_scalar_prefetch=1, grid=(nb, nsteps), in_specs=in_specs,
        out_specs=(pl.BlockSpec((None, ts, wv), lambda b, s, pt_ref: (b, 0, 0)),
                   pl.BlockSpec((None, rm, KV_LORA), lambda b, s, pt_ref: (b, 0, 0))),
        scratch_shapes=[pltpu.VMEM((rd, LANES), F32), pltpu.VMEM((rd, LANES), F32), pltpu.VMEM((rd, wv), F32),
                        pltpu.VMEM((rm, LANES), F32), pltpu.VMEM((rm, LANES), F32), pltpu.VMEM((rm, KV_LORA), F32)])
    return pl.pallas_call(
        functools.partial(_dec_even_kernel, npg=npg, ts=ts, lam_init=lam_init),
        grid_spec=grid_spec,
        out_shape=(jax.ShapeDtypeStruct((nb, ts, wv), F32), jax.ShapeDtypeStruct((nb, rm, KV_LORA), F32)),
        compiler_params=_cparams(("parallel", "arbitrary")), name="dec_even",
    )(ptf, qd, ql, qr, own_k, own_v, own_m, lam_vec, gsub, *([ck] * npg), *([cv] * npg), *([cm] * npg))


def _uv_kernel(ol_ref, wuv_ref, o_ref):
    o_ref[...] = _dot(ol_ref[...].astype(BF16), wuv_ref[...]).astype(o_ref.dtype)


def _uv(ol, wuv):
    _, r, _ = ol.shape
    return pl.pallas_call(
        _uv_kernel, grid=(H_B,),
        in_specs=[pl.BlockSpec((None, r, KV_LORA), lambda h: (h, 0, 0)),
                  pl.BlockSpec((None, KV_LORA, V_B), lambda h: (h, 0, 0))],
        out_specs=pl.BlockSpec((r, V_B), lambda h: (0, h)),
        out_shape=jax.ShapeDtypeStruct((r, H_B * V_B), BF16),
        compiler_params=_cparams(("parallel",)), name="mla_uv",
    )(ol, wuv)


def _dec_sb_kernel(pt_ref, q_ref, ok_ref, ov_ref, u_ref, *rest, npg, ts):
    kp = rest[0:npg]
    vp = rest[npg:2 * npg]
    o_ref = rest[2 * npg]
    c_ref, acc_ref = rest[2 * npg + 1:]
    s_id = pl.program_id(1)
    q = q_ref[...]
    u = u_ref[...]

    @pl.when(s_id == 0)
    def _():
        c_ref[...] = jnp.zeros(c_ref.shape, F32)
        acc_ref[...] = jnp.zeros(acc_ref.shape, F32)
        key = lax.broadcasted_iota(jnp.int32, (1, PAGE), 1)
        tok = lax.broadcasted_iota(jnp.int32, (H_C * ts, 1), 0) % ts
        z = _dot_nt(q, ok_ref[...].astype(BF16))
        _sb_step(z, ov_ref[...].astype(BF16), u, key < tok, c_ref, acc_ref)

    for p in range(npg):
        z = _dot_nt(q, kp[p][...].astype(BF16))
        _sb_step(z, vp[p][...].astype(BF16), u, None, c_ref, acc_ref)

    @pl.when(s_id == pl.num_programs(1) - 1)
    def _():
        acc = acc_ref[...]
        rows = G_C * ts
        for kh in range(KVH_C):
            o_ref[kh] = acc[kh * rows:(kh + 1) * rows, kh * DH_C:(kh + 1) * DH_C]


def _dec_sb(pt, q, own_k, own_v, ck, cv, layer, npg):
    nb, n_pages = pt.shape
    ts = q.shape[1] // H_C
    nsteps = n_pages // npg
    ptf = pt.reshape(-1)
    w = ck.shape[-1]
    u = _tri(PAGE)

    def page_spec(p):
        return pl.BlockSpec((None, None, PAGE, w),
                            lambda b, s, pt_ref: (layer, pt_ref[b * n_pages + n_pages - 1 - (s * npg + p)], 0, 0))

    per_b = lambda a: pl.BlockSpec((None,) + a.shape[1:], lambda b, s, pt_ref: (b,) + (0,) * (a.ndim - 1))
    in_specs = ([per_b(q), per_b(own_k), per_b(own_v), pl.BlockSpec(u.shape, lambda b, s, pt_ref: (0, 0))]
                + [page_spec(p) for p in range(npg)] + [page_spec(p) for p in range(npg)])
    rq = q.shape[1]
    grid_spec = pltpu.PrefetchScalarGridSpec(
        num_scalar_prefetch=1, grid=(nb, nsteps), in_specs=in_specs,
        out_specs=pl.BlockSpec((None, KVH_C, G_C * ts, DH_C), lambda b, s, pt_ref: (b, 0, 0, 0)),
        scratch_shapes=[pltpu.VMEM((rq, LANES), F32), pltpu.VMEM((rq, w), F32)])
    return pl.pallas_call(
        functools.partial(_dec_sb_kernel, npg=npg, ts=ts),
        grid_spec=grid_spec,
        out_shape=jax.ShapeDtypeStruct((nb, KVH_C, G_C * ts, DH_C), F32),
        compiler_params=_cparams(("parallel", "arbitrary")), name="dec_sb",
    )(ptf, q, own_k, own_v, u, *([ck] * npg), *([cv] * npg))


def _pad_rows(a, n):
    return jnp.pad(a, ((0, 0), (0, n - a.shape[1]), (0, 0)))


def _block_diag_queries(q, groups, width):
    nb, ts, _ = q.shape
    eye = jnp.eye(groups, dtype=q.dtype)
    qg = q.reshape(nb, ts, groups, width)
    out = jnp.einsum('btgw,hg->bhtgw', qg, eye)
    return out.reshape(nb, groups * ts, groups * width)


def kernel(x_prompt, x_sample, cache_diff_k, cache_diff_v, cache_mla, cache_sb_k, cache_sb_v, page_table,
           g_mix, g_ffn, w_in_even, diff_lambda, g_diff_sub, g_mla_q, g_mla_kv, w_mla_uq, w_mla_uk, w_mla_uv,
           w_out_even, w_in_odd, w_out_odd, w_ff1, w_ff2, g_final):
    b, t, d = x_prompt.shape
    nb, ts, _ = x_sample.shape
    n_pages = page_table.shape[1]
    past = n_pages * PAGE
    depth = g_mix.shape[0]
    n_even, n_odd = (depth + 1) // 2, depth // 2
    n_pool = cache_diff_k.shape[1]
    npg = _tile(n_pages, 4)

    xp = x_prompt.reshape(b * t, d)
    xs = x_sample.reshape(nb * ts, d)
    tm_p = _tile(b * t, 512)
    tm_s = _tile(nb * ts, 512)
    tab_p = _rope_tables(jnp.arange(t))
    tab_s = jnp.tile(_rope_tables(past + jnp.arange(ts)), (tm_s // ts, 1))

    ck = cache_diff_k.reshape(n_even, n_pool, PAGE, N_A)
    cv = cache_diff_v.reshape(n_even, n_pool, PAGE, N_A)
    csk = cache_sb_k.reshape(n_odd, n_pool, PAGE, NK_C)
    csv = cache_sb_v.reshape(n_odd, n_pool, PAGE, NK_C)

    row2 = lambda v: v.reshape(1, -1)
    outs_p = {k: [] for k in ("dk", "dv", "ml", "sk", "sv")}
    outs_s = {k: [] for k in ("dk", "dv", "ml", "sk", "sv")}

    for l in range(depth):
        w1 = w_ff1[l].astype(BF16)
        w2 = w_ff2[l].astype(BF16)
        final = l == depth - 1
        gf = row2(g_final)
        if l % 2 == 0:
            e = l // 2
            lam_init = 0.8 - 0.6 * math.exp(-0.3 * l)
            wi = w_in_even[e]
            c0 = 3 * N_A + Q_LORA + KV_LORA
            w_all = jnp.concatenate([wi[:, :c0]] + [wi[:, c0:]] * H_B, axis=1).astype(BF16)
            uq = w_mla_uq[e].reshape(Q_LORA, H_B, NOPE_B + ROPE_B)
            wuq = jnp.concatenate([uq[:, :, :NOPE_B].reshape(Q_LORA, H_B * NOPE_B),
                                   uq[:, :, NOPE_B:].reshape(Q_LORA, H_B * ROPE_B)], axis=1).astype(BF16)
            wuk = jnp.einsum('hcn,hg->hngc', w_mla_uk[e], jnp.eye(H_B, dtype=F32)).reshape(
                H_B * NOPE_B, H_B * KV_LORA).astype(BF16)
            wuv = w_mla_uv[e].astype(BF16)
            wo = w_out_even[e].astype(BF16)
            gq, gkv, gsub = row2(g_mla_q[e]), row2(g_mla_kv[e]), row2(g_diff_sub[e])
            lam_vec = diff_lambda[e]
            gm = row2(g_mix[l])

            kd, vd, ml, qd_b, kd_b, vd_b, qm_b, km_b = _even_proj(xp, gm, w_all, tab_p, gq, gkv, wuq, wuk, tm_p)
            outs_p["dk"].append(kd.reshape(b, t, H_A, 2, DH_A))
            outs_p["dv"].append(vd.reshape(b, t, H_A, 2 * DH_A))
            outs_p["ml"].append(ml.reshape(b, t, KV_LORA + ROPE_B))
            od = _diff_attn(qd_b, kd_b, vd_b, lam_vec, gsub, b, t, lam_init)
            om = _mla_attn(qm_b, km_b, wuv, b, t)
            xp = _post(xp, od, 0, om, 0, wo, row2(g_ffn[l]), w1, w2, gf, final)

            kd, vd, ml, qd_b, _, _, qm_b, _ = _even_proj(xs, gm, w_all, tab_s, gq, gkv, wuq, wuk, tm_s)
            outs_s["dk"].append(kd.reshape(nb, ts, H_A, 2, DH_A))
            outs_s["dv"].append(vd.reshape(nb, ts, H_A, 2 * DH_A))
            outs_s["ml"].append(ml.reshape(nb, ts, KV_LORA + ROPE_B))
            qbd = _block_diag_queries(qd_b.reshape(nb, ts, N_A), H_A * 2, DH_A)
            qm4 = qm_b.reshape(H_B, nb, ts, MLA_W)
            ql = jnp.transpose(qm4[..., :KV_LORA], (1, 0, 2, 3)).reshape(nb, H_B * ts, KV_LORA)
            qr = jnp.stack([qm4[h, :, :, KV_LORA + h * ROPE_B:KV_LORA + (h + 1) * ROPE_B] for h in range(H_B)],
                           axis=1).reshape(nb, H_B * ts, ROPE_B)
            od_s, ol_s = _dec_even(page_table, qbd, ql, qr,
                                   _pad_rows(kd.reshape(nb, ts, N_A), PAGE),
                                   _pad_rows(vd.reshape(nb, ts, N_A), PAGE),
                                   _pad_rows(ml.reshape(nb, ts, KV_LORA + ROPE_B), PAGE),
                                   lam_vec, gsub, ck, cv, cache_mla, e, npg, lam_init)
            ol_h = jnp.transpose(ol_s.reshape(nb, H_B, ts, KV_LORA), (1, 0, 2, 3)).reshape(H_B, nb * ts, KV_LORA)
            om_s = _uv(ol_h, wuv)
            xs = _post(xs, od_s.reshape(nb * ts, N_A), 0, om_s, 0, wo, row2(g_ffn[l]), w1, w2, gf, final)
        else:
            o = l // 2
            wi = w_in_odd[o]
            wq, wk, wv = wi[:, :NQ_C], wi[:, NQ_C:NQ_C + NK_C], wi[:, NQ_C + NK_C:]
            dup = lambda w: jnp.repeat(w.reshape(d, KVH_C, 1, DH_C), 2, axis=2).reshape(d, 2 * NK_C)
            w_all = jnp.concatenate([wq, wk, wv, dup(wk), dup(wv)], axis=1).astype(BF16)
            wo = w_out_odd[o].astype(BF16)
            gm = row2(g_mix[l])

            k, v, q_b, k2_b, v2_b = _odd_proj(xp, gm, w_all, tm_p)
            outs_p["sk"].append(k.reshape(b, t, KVH_C, DH_C))
            outs_p["sv"].append(v.reshape(b, t, KVH_C, DH_C))
            att = _sb_attn(q_b, k2_b, v2_b, b, t)
            xp = _post(xp, att, 0, att, 1, wo, row2(g_ffn[l]), w1, w2, gf, final)

            k, v, q_b, _, _ = _odd_proj(xs, gm, w_all, tm_s)
            outs_s["sk"].append(k.reshape(nb, ts, KVH_C, DH_C))
            outs_s["sv"].append(v.reshape(nb, ts, KVH_C, DH_C))
            qg = jnp.transpose(q_b.reshape(nb, ts, KVH_C, G_C, DH_C), (0, 2, 3, 1, 4))
            qbd = jnp.einsum('bkgtd,kj->bkgtjd', qg, jnp.eye(KVH_C, dtype=BF16)).reshape(
                nb, H_C * ts, NK_C)
            att_s = _dec_sb(page_table, qbd, _pad_rows(k.reshape(nb, ts, NK_C), PAGE),
                            _pad_rows(v.reshape(nb, ts, NK_C), PAGE), csk, csv, o, npg)
            att_s = jnp.transpose(att_s.reshape(nb, KVH_C, G_C, ts, DH_C), (0, 3, 1, 2, 4)).reshape(nb * ts, NQ_C)
            xs = _post(xs, att_s, 0, att_s, 1, wo, row2(g_ffn[l]), w1, w2, gf, final)

    st = lambda xs_: jnp.stack(xs_)
    return (xp.reshape(b, t, d), xs.reshape(nb, ts, d),
            st(outs_p["dk"]), st(outs_p["dv"]), st(outs_p["ml"]), st(outs_p["sk"]), st(outs_p["sv"]),
            st(outs_s["dk"]), st(outs_s["dv"]), st(outs_s["ml"]), st(outs_s["sk"]), st(outs_s["sv"]))
```

```python
import functools
import math

import jax
import jax.numpy as jnp
from jax import lax
from jax.experimental import pallas as pl
from jax.experimental.pallas import tpu as pltpu

F32 = jnp.float32
BF16 = jnp.bfloat16

EPS = 1e-6
NEG = -1e30
H_A = 4
DH_A = 64
ROT_A = DH_A // 4
ROPE_THETA = 500000.0
SCALE_A = DH_A ** -0.5
H_B = 4
Q_LORA = 384
KV_LORA = 256
NOPE_B = 64
ROPE_B = 32
V_B = 128
MLA_THETA = 10000.0
SCALE_B = (NOPE_B + ROPE_B) ** -0.5
H_C = 16
KVH_C = 4
G_C = H_C // KVH_C
DH_C = 64
SCALE_C = DH_C ** -0.5
PAGE = 128

LANES = 128
VMEM_LIMIT = 48 * 1024 * 1024

N_A = H_A * 2 * DH_A
MLA_W = KV_LORA + LANES


def _tile(n, pref):
    t = min(n, pref)
    while n % t:
        t //= 2
    return t


def _dot(a, b):
    return jnp.dot(a, b, preferred_element_type=F32)


def _dot_nt(a, b):
    return lax.dot_general(a, b, (((1,), (1,)), ((), ())), preferred_element_type=F32)


def _rep(x, width):
    n = width // LANES
    return x if n == 1 else pltpu.repeat(x, n, 1)


def _rms(x, g):
    return x * lax.rsqrt(jnp.mean(x * x, axis=-1, keepdims=True) + EPS) * g


def _cparams(sem, vmem=VMEM_LIMIT):
    return pltpu.CompilerParams(dimension_semantics=sem, vmem_limit_bytes=vmem)


def _rope_tables(pos):
    posf = pos.astype(F32)[:, None]
    lane = jnp.arange(LANES)

    def tab(period, half, theta):
        inv = theta ** (-jnp.arange(half, dtype=F32) / half)
        ang = posf * inv[None, :]
        cos, sin = jnp.cos(ang), jnp.sin(ang)
        d = lane % period
        first = d < half
        second = (d >= half) & (d < 2 * half)
        idx = jnp.where(first, d, jnp.where(second, d - half, 0))
        c = jnp.where((first | second)[None, :], cos[:, idx], 1.0)
        s1 = jnp.where(first[None, :], -sin[:, idx], 0.0)
        s2 = jnp.where(second[None, :], sin[:, idx], 0.0)
        return [c, s1, s2]

    return jnp.concatenate(tab(DH_A, ROT_A // 2, ROPE_THETA) + tab(ROPE_B, ROPE_B // 2, MLA_THETA), axis=1)


def _rope(y, c, s1, s2, half):
    outs = []
    for k in range(y.shape[1] // LANES):
        yb = y[:, k * LANES:(k + 1) * LANES]
        outs.append(yb * c + pltpu.roll(yb, LANES - half, 1) * s1 + pltpu.roll(yb, half, 1) * s2)
    return outs[0] if len(outs) == 1 else jnp.concatenate(outs, axis=1)


def _even_proj_kernel(x_ref, g_ref, w_ref, tab_ref, gq_ref, gkv_ref, wuq_ref, wuk_ref,
                      kd_o, vd_o, mla_o, qd_b, kd_b, vd_b, qm_b, km_b):
    hb = _rms(x_ref[...], g_ref[...]).astype(BF16)
    tab = tab_ref[...]
    ca, s1a, s2a, cb, s1b, s2b = [tab[:, k * LANES:(k + 1) * LANES] for k in range(6)]
    o = 0
    qd = _rope(_dot(hb, w_ref[:, o:o + N_A]), ca, s1a, s2a, ROT_A // 2)
    qd_b[...] = (qd * SCALE_A).astype(BF16)
    o += N_A
    kd = _rope(_dot(hb, w_ref[:, o:o + N_A]), ca, s1a, s2a, ROT_A // 2)
    kd_o[...] = kd
    kd_b[...] = kd.astype(BF16)
    o += N_A
    vd = _dot(hb, w_ref[:, o:o + N_A])
    vd_o[...] = vd
    vd_b[...] = vd.astype(BF16)
    o += N_A
    cq = _dot(hb, w_ref[:, o:o + Q_LORA])
    o += Q_LORA
    ckv = _dot(hb, w_ref[:, o:o + KV_LORA])
    o += KV_LORA
    kr4 = _dot(hb, w_ref[:, o:o + LANES])
    cqn = _rms(cq, gq_ref[...]).astype(BF16)
    qn = _dot(cqn, wuq_ref[:, 0:H_B * NOPE_B]).astype(BF16)
    qr = _dot(cqn, wuq_ref[:, H_B * NOPE_B:])
    qlat = _dot(qn, wuk_ref[...])
    qr = _rope(qr, cb, s1b, s2b, ROPE_B // 2)
    ckvn = _rms(ckv, gkv_ref[...])
    kr4 = _rope(kr4, cb, s1b, s2b, ROPE_B // 2)
    mla_o[:, 0:KV_LORA] = ckvn
    mla_o[:, KV_LORA:KV_LORA + ROPE_B] = kr4[:, 0:ROPE_B]
    km_b[:, 0:KV_LORA] = ckvn.astype(BF16)
    km_b[:, KV_LORA:] = kr4.astype(BF16)
    lane = lax.broadcasted_iota(jnp.int32, (1, LANES), 1)
    for h in range(H_B):
        qm_b[h, :, 0:KV_LORA] = (qlat[:, h * KV_LORA:(h + 1) * KV_LORA] * SCALE_B).astype(BF16)
        qm_b[h, :, KV_LORA:] = jnp.where((lane >= h * ROPE_B) & (lane < (h + 1) * ROPE_B), qr * SCALE_B, 0.0).astype(BF16)


def _even_proj(x, g, w, tab, gq, gkv, wuq, wuk, tm):
    r, d = x.shape
    nt = tab.shape[0] // tm
    wcols = w.shape[1]
    row = lambda n: pl.BlockSpec((tm, n), lambda i: (i, 0))
    full = lambda a: pl.BlockSpec(a.shape, lambda i: (0,) * a.ndim)
    out_shape = (
        jax.ShapeDtypeStruct((r, N_A), F32), jax.ShapeDtypeStruct((r, N_A), F32),
        jax.ShapeDtypeStruct((r, KV_LORA + ROPE_B), F32),
        jax.ShapeDtypeStruct((r, N_A), BF16), jax.ShapeDtypeStruct((r, N_A), BF16),
        jax.ShapeDtypeStruct((r, N_A), BF16),
        jax.ShapeDtypeStruct((H_B, r, MLA_W), BF16), jax.ShapeDtypeStruct((r, MLA_W), BF16))
    out_specs = (row(N_A), row(N_A), row(KV_LORA + ROPE_B), row(N_A), row(N_A), row(N_A),
                 pl.BlockSpec((H_B, tm, MLA_W), lambda i: (0, i, 0)), row(MLA_W))
    return pl.pallas_call(
        _even_proj_kernel, grid=(r // tm,),
        in_specs=[row(d), full(g), full(w), pl.BlockSpec((tm, tab.shape[1]), lambda i: (i % nt, 0)),
                  full(gq), full(gkv), full(wuq), full(wuk)],
        out_specs=out_specs, out_shape=out_shape,
        compiler_params=_cparams(("parallel",)), name="even_proj",
    )(x, g, w, tab, gq, gkv, wuq, wuk)


def _lam(lam_ref, lam_init):
    lf = lam_ref[...]
    a = jnp.sum(lf[0:1] * lf[1:2], axis=-1, keepdims=True)
    b = jnp.sum(lf[2:3] * lf[3:4], axis=-1, keepdims=True)
    return jnp.exp(a) - jnp.exp(b) + lam_init


def _softmax_step(s, vb, m_ref, l_ref, acc_ref):
    m_prev = m_ref[...]
    m_new = jnp.maximum(m_prev, jnp.max(s, axis=-1, keepdims=True))
    alpha = jnp.exp(m_prev - m_new)
    p = jnp.exp(s - _rep(m_new, s.shape[1]))
    l_ref[...] = alpha * l_ref[...] + jnp.sum(p, axis=-1, keepdims=True)
    pv = _dot(p.astype(BF16), vb)
    acc_ref[...] = _rep(alpha, acc_ref.shape[1]) * acc_ref[...] + pv
    m_ref[...] = m_new


def _diff_attn_kernel(q_ref, k_ref, v_ref, lam_ref, gsub_ref, o_ref, qq_ref, m_ref, l_ref, acc_ref,
                      *, tq, tk, lam_init):
    i = pl.program_id(2)
    lane = lax.broadcasted_iota(jnp.int32, (1, LANES), 1)
    q = q_ref[...]
    zero = jnp.zeros_like(q)
    qq_ref[0:tq, :] = jnp.where(lane < DH_A, q, zero)
    qq_ref[tq:2 * tq, :] = jnp.where(lane >= DH_A, q, zero)
    m_ref[...] = jnp.full(m_ref.shape, NEG, F32)
    l_ref[...] = jnp.zeros(l_ref.shape, F32)
    acc_ref[...] = jnp.zeros(acc_ref.shape, F32)

    def block(off, mask):
        s = _dot_nt(qq_ref[...], k_ref[pl.ds(off, tk), :])
        if mask is not None:
            s = jnp.where(mask, s, NEG)
        _softmax_step(s, v_ref[pl.ds(off, tk), :], m_ref, l_ref, acc_ref)

    def body(j, carry):
        block(pl.multiple_of(j * tk, tk), None)
        return carry

    lax.fori_loop(0, i * (tq // tk), body, 0)
    rows = lax.broadcasted_iota(jnp.int32, (2 * tq, 1), 0) % tq
    cols = lax.broadcasted_iota(jnp.int32, (1, tk), 1)
    for d in range(tq // tk):
        block(pl.multiple_of(i * tq + d * tk, tk), (cols + d * tk) <= rows)

    o = acc_ref[...] / l_ref[...]
    od = o[0:tq] - _lam(lam_ref, lam_init) * o[tq:2 * tq]
    o_ref[...] = (_rms(od, gsub_ref[...]) * (1.0 - lam_init)).astype(o_ref.dtype)


def _diff_attn(qd, kd, vd, lam_vec, gsub, b, t, lam_init):
    tq = _tile(t, 1024)
    tk = _tile(tq, 1024)
    nq = t // tq
    return pl.pallas_call(
        functools.partial(_diff_attn_kernel, tq=tq, tk=tk, lam_init=lam_init),
        grid=(b, H_A, nq),
        in_specs=[pl.BlockSpec((tq, LANES), lambda bb, h, i: (bb * nq + i, h)),
                  pl.BlockSpec((t, LANES), lambda bb, h, i: (bb, h)),
                  pl.BlockSpec((t, LANES), lambda bb, h, i: (bb, h)),
                  pl.BlockSpec(lam_vec.shape, lambda bb, h, i: (0, 0)),
                  pl.BlockSpec(gsub.shape, lambda bb, h, i: (0, 0))],
        out_specs=pl.BlockSpec((tq, LANES), lambda bb, h, i: (bb * nq + i, h)),
        out_shape=jax.ShapeDtypeStruct((b * t, N_A), BF16),
        scratch_shapes=[pltpu.VMEM((2 * tq, LANES), BF16), pltpu.VMEM((2 * tq, LANES), F32),
                        pltpu.VMEM((2 * tq, LANES), F32), pltpu.VMEM((2 * tq, LANES), F32)],
        compiler_params=_cparams(("parallel", "parallel", "arbitrary")), name="diff_attn",
    )(qd, kd, vd, lam_vec, gsub)


def _mla_attn_kernel(q_ref, k_ref, wuv_ref, o_ref, qq_ref, m_ref, l_ref, acc_ref, *, tq, tk):
    i = pl.program_id(1)
    for h in range(H_B):
        qq_ref[h * tq:(h + 1) * tq, :] = q_ref[h]
    m_ref[...] = jnp.full(m_ref.shape, NEG, F32)
    l_ref[...] = jnp.zeros(l_ref.shape, F32)
    acc_ref[...] = jnp.zeros(acc_ref.shape, F32)

    def block(off, mask):
        kb = k_ref[pl.ds(off, tk), :]
        s = _dot_nt(qq_ref[...], kb)
        if mask is not None:
            s = jnp.where(mask, s, NEG)
        _softmax_step(s, kb[:, 0:KV_LORA], m_ref, l_ref, acc_ref)

    def body(j, carry):
        block(pl.multiple_of(j * tk, tk), None)
        return carry

    lax.fori_loop(0, i * (tq // tk), body, 0)
    rows = lax.broadcasted_iota(jnp.int32, (H_B * tq, 1), 0) % tq
    cols = lax.broadcasted_iota(jnp.int32, (1, tk), 1)
    for d in range(tq // tk):
        block(pl.multiple_of(i * tq + d * tk, tk), (cols + d * tk) <= rows)

    ol = (acc_ref[...] / _rep(l_ref[...], KV_LORA)).astype(BF16)
    for h in range(H_B):
        o_ref[:, h * V_B:(h + 1) * V_B] = _dot(ol[h * tq:(h + 1) * tq], wuv_ref[h]).astype(o_ref.dtype)


def _mla_attn(qm, km, wuv, b, t):
    tq = _tile(t, 512)
    tk = _tile(tq, 512)
    nq = t // tq
    return pl.pallas_call(
        functools.partial(_mla_attn_kernel, tq=tq, tk=tk),
        grid=(b, nq),
        in_specs=[pl.BlockSpec((H_B, tq, MLA_W), lambda bb, i: (0, bb * nq + i, 0)),
                  pl.BlockSpec((t, MLA_W), lambda bb, i: (bb, 0)),
                  pl.BlockSpec(wuv.shape, lambda bb, i: (0, 0, 0))],
        out_specs=pl.BlockSpec((tq, H_B * V_B), lambda bb, i: (bb * nq + i, 0)),
        out_shape=jax.ShapeDtypeStruct((b * t, H_B * V_B), BF16),
        scratch_shapes=[pltpu.VMEM((H_B * tq, MLA_W), BF16), pltpu.VMEM((H_B * tq, LANES), F32),
                        pltpu.VMEM((H_B * tq, LANES), F32), pltpu.VMEM((H_B * tq, KV_LORA), F32)],
        compiler_params=_cparams(("parallel", "arbitrary")), name="mla_attn",
    )(qm, km, wuv)


def _post_kernel(x_ref, a1_ref, a2_ref, wo_ref, g_ref, w1_ref, w2_ref, gf_ref, o_ref,
                 x1_ref, hn_ref, acc_ref, *, final):
    j = pl.program_id(1)
    half = a1_ref.shape[1]

    @pl.when(j == 0)
    def _():
        mix = (_dot(a1_ref[...].astype(BF16), wo_ref[0:half, :])
               + _dot(a2_ref[...].astype(BF16), wo_ref[half:2 * half, :]))
        x1 = x_ref[...] + mix
        x1_ref[...] = x1
        hn_ref[...] = _rms(x1, g_ref[...]).astype(BF16)
        acc_ref[...] = jnp.zeros(acc_ref.shape, F32)

    h1 = jnp.maximum(_dot(hn_ref[...], w1_ref[...]), 0.0)
    acc_ref[...] += _dot((h1 * h1).astype(BF16), w2_ref[...])

    @pl.when(j == pl.num_programs(1) - 1)
    def _():
        y = x1_ref[...] + acc_ref[...]
        if final:
            y = _rms(y, gf_ref[...])
        o_ref[...] = y


def _post(x, a1, c1, a2, c2, wo, g, w1, w2, gf, final):
    r, d = x.shape
    tm = _tile(r, 512)
    dff = w1.shape[1]
    tf = _tile(dff, 1024)
    half = d // 2
    return pl.pallas_call(
        functools.partial(_post_kernel, final=final),
        grid=(r // tm, dff // tf),
        in_specs=[pl.BlockSpec((tm, d), lambda i, j: (i, 0)),
                  pl.BlockSpec((tm, half), lambda i, j: (i, c1)),
                  pl.BlockSpec((tm, half), lambda i, j: (i, c2)),
                  pl.BlockSpec(wo.shape, lambda i, j: (0, 0)),
                  pl.BlockSpec(g.shape, lambda i, j: (0, 0)),
                  pl.BlockSpec((d, tf), lambda i, j: (0, j)),
                  pl.BlockSpec((tf, d), lambda i, j: (j, 0)),
                  pl.BlockSpec(gf.shape, lambda i, j: (0, 0))],
        out_specs=pl.BlockSpec((tm, d), lambda i, j: (i, 0)),
        out_shape=jax.ShapeDtypeStruct((r, d), F32),
        scratch_shapes=[pltpu.VMEM((tm, d), F32), pltpu.VMEM((tm, d), BF16), pltpu.VMEM((tm, d), F32)],
        compiler_params=_cparams(("parallel", "arbitrary")), name="post_final" if final else "post",
    )(x, a1, a2, wo, g, w1, w2, gf)


NQ_C = H_C * DH_C
NK_C = KVH_C * DH_C


def _odd_proj_kernel(x_ref, g_ref, w_ref, k_o, v_o, q_b, k2_b, v2_b):
    hb = _rms(x_ref[...], g_ref[...]).astype(BF16)
    o = 0
    q_b[...] = (_dot(hb, w_ref[:, o:o + NQ_C]) * SCALE_C).astype(BF16)
    o += NQ_C
    k_o[...] = _dot(hb, w_ref[:, o:o + NK_C])
    o += NK_C
    v_o[...] = _dot(hb, w_ref[:, o:o + NK_C])
    o += NK_C
    k2_b[...] = _dot(hb, w_ref[:, o:o + 2 * NK_C]).astype(BF16)
    o += 2 * NK_C
    v2_b[...] = _dot(hb, w_ref[:, o:o + 2 * NK_C]).astype(BF16)


def _odd_proj(x, g, w, tm):
    r, d = x.shape
    row = lambda n: pl.BlockSpec((tm, n), lambda i: (i, 0))
    full = lambda a: pl.BlockSpec(a.shape, lambda i: (0,) * a.ndim)
    return pl.pallas_call(
        _odd_proj_kernel, grid=(r // tm,),
        in_specs=[row(d), full(g), full(w)],
        out_specs=(row(NK_C), row(NK_C), row(NQ_C), row(2 * NK_C), row(2 * NK_C)),
        out_shape=(jax.ShapeDtypeStruct((r, NK_C), F32), jax.ShapeDtypeStruct((r, NK_C), F32),
                   jax.ShapeDtypeStruct((r, NQ_C), BF16), jax.ShapeDtypeStruct((r, 2 * NK_C), BF16),
                   jax.ShapeDtypeStruct((r, 2 * NK_C), BF16)),
        compiler_params=_cparams(("parallel",)), name="odd_proj",
    )(x, g, w)


SB_DONE = -104.0


def _sb_live(c_ref):
    return jnp.max(c_ref[...]) > SB_DONE


def _sb_step(z, vb, u, mask, c_ref, acc_ref, v_feature_major=False):
    tk = z.shape[1]
    ls = jnp.minimum(z, 0.0) - jnp.log(1.0 + jnp.exp(-jnp.abs(z)))
    lneg = ls - z
    if mask is not None:
        lneg = jnp.where(mask, lneg, 0.0)
    hi = lneg.astype(BF16)
    lo = (lneg - hi.astype(F32)).astype(BF16)
    suf = _dot(hi, u) + _dot(lo, u)
    c = c_ref[...]
    a = jnp.exp(ls + suf + _rep(c, tk))
    if mask is not None:
        a = jnp.where(mask, a, 0.0)
    ab = a.astype(BF16)
    acc_ref[...] += _dot_nt(ab, vb) if v_feature_major else _dot(ab, vb)
    c_ref[...] = c + (suf[:, 0:1] + lneg[:, 0:1])


def _sb_attn_kernel(q_ref, k_ref, v_ref, u_ref, o_ref, qq_ref, c_ref, acc_ref, *, tq, tk):
    i = pl.program_id(2)
    lane = lax.broadcasted_iota(jnp.int32, (1, LANES), 1)
    for g in range(G_C):
        qb = q_ref[:, (g // 2) * LANES:(g // 2 + 1) * LANES]
        keep = (lane < DH_C) if g % 2 == 0 else (lane >= DH_C)
        qq_ref[g * tq:(g + 1) * tq, :] = jnp.where(keep, qb, jnp.zeros_like(qb))
    c_ref[...] = jnp.zeros(c_ref.shape, F32)
    acc_ref[...] = jnp.zeros(acc_ref.shape, F32)

    def block(off, mask):
        z = _dot_nt(qq_ref[...], k_ref[pl.ds(off, tk), :])
        _sb_step(z, v_ref[pl.ds(off, tk), :], u_ref[...], mask, c_ref, acc_ref)

    rows = lax.broadcasted_iota(jnp.int32, (G_C * tq, 1), 0) % tq
    cols = lax.broadcasted_iota(jnp.int32, (1, tk), 1)
    for d in reversed(range(tq // tk)):
        block(pl.multiple_of(i * tq + d * tk, tk), (cols + d * tk) < rows)

    def cond(carry):
        j, live = carry
        return jnp.logical_and(j >= 0, live)

    def body(carry):
        j, _ = carry
        block(pl.multiple_of(j * tk, tk), None)
        return j - 1, _sb_live(c_ref)

    lax.while_loop(cond, body, (i * (tq // tk) - 1, _sb_live(c_ref)))
    acc = acc_ref[...]
    for c2 in range(G_C // 2):
        o_ref[:, c2 * LANES:(c2 + 1) * LANES] = jnp.where(
            lane < DH_C, acc[(2 * c2) * tq:(2 * c2 + 1) * tq], acc[(2 * c2 + 1) * tq:(2 * c2 + 2) * tq]
        ).astype(o_ref.dtype)


def _tri(tk):
    j = jnp.arange(tk)
    return (j[:, None] > j[None, :]).astype(BF16)


def _sb_attn(q, k2, v2, b, t):
    tq = _tile(t, 256)
    tk = _tile(tq, 256)
    nq = t // tq
    u = _tri(tk)
    return pl.pallas_call(
        functools.partial(_sb_attn_kernel, tq=tq, tk=tk),
        grid=(b, KVH_C, nq),
        in_specs=[pl.BlockSpec((tq, G_C * DH_C), lambda bb, h, i: (bb * nq + i, h)),
                  pl.BlockSpec((t, LANES), lambda bb, h, i: (bb, h)),
                  pl.BlockSpec((t, LANES), lambda bb, h, i: (bb, h)),
                  pl.BlockSpec(u.shape, lambda bb, h, i: (0, 0))],
        out_specs=pl.BlockSpec((tq, G_C * DH_C), lambda bb, h, i: (bb * nq + i, h)),
        out_shape=jax.ShapeDtypeStruct((b * t, NQ_C), BF16),
        scratch_shapes=[pltpu.VMEM((G_C * tq, LANES), BF16), pltpu.VMEM((G_C * tq, LANES), F32),
                        pltpu.VMEM((G_C * tq, LANES), F32)],
        compiler_params=_cparams(("parallel", "parallel", "arbitrary")), name="sb_attn",
    )(q, k2, v2, u)


def _dec_even_kernel(pt_ref, qd_ref, ql_ref, qr_ref, ok_ref, ov_ref, om_ref, lam_ref, gsub_ref, *rest,
                     npg, ts, lam_init):
    kp = rest[0:npg]
    vp = rest[npg:2 * npg]
    mp = rest[2 * npg:3 * npg]
    od_ref, ol_ref = rest[3 * npg:3 * npg + 2]
    md_ref, ld_ref, accd_ref, mm_ref, lm_ref, accm_ref = rest[3 * npg + 2:]
    s_id = pl.program_id(1)
    rows_h = 2 * ts

    @pl.when(s_id == 0)
    def _():
        md_ref[...] = jnp.full(md_ref.shape, NEG, F32)
        ld_ref[...] = jnp.zeros(ld_ref.shape, F32)
        accd_ref[...] = jnp.zeros(accd_ref.shape, F32)
        mm_ref[...] = jnp.full(mm_ref.shape, NEG, F32)
        lm_ref[...] = jnp.zeros(lm_ref.shape, F32)
        accm_ref[...] = jnp.zeros(accm_ref.shape, F32)

    def softmax(s, m_ref, l_ref):
        m_prev = m_ref[...]
        m_new = jnp.maximum(m_prev, jnp.max(s, axis=-1, keepdims=True))
        alpha = jnp.exp(m_prev - m_new)
        p = jnp.exp(s - _rep(m_new, s.shape[1]))
        l_ref[...] = alpha * l_ref[...] + jnp.sum(p, axis=-1, keepdims=True)
        m_ref[...] = m_new
        return p.astype(BF16), alpha

    def cat(xs, axis):
        return xs[0] if len(xs) == 1 else jnp.concatenate(xs, axis=axis)

    def update(ks, vs, ms, dmask, mmask):
        qd = qd_ref[...]
        sd = cat([_dot(qd, k[...].astype(BF16)) for k in ks], 1)
        if dmask is not None:
            sd = jnp.where(dmask, sd, NEG)
        pd, alpha = softmax(sd, md_ref, ld_ref)
        for h in range(H_A):
            vh = cat([v[pl.ds(h, PAGE, stride=H_A), :].astype(BF16) for v in vs], 0)
            r = slice(h * rows_h, (h + 1) * rows_h)
            accd_ref[r, :] = alpha[r] * accd_ref[r, :] + _dot(pd[r], vh)
        ql = ql_ref[...]
        qr = qr_ref[...]
        mb = [m[...].astype(BF16) for m in ms]
        sm = cat([_dot(ql, m[0:KV_LORA]) + _dot(qr, m[KV_LORA:KV_LORA + ROPE_B]) for m in mb], 1)
        if mmask is not None:
            sm = jnp.where(mmask, sm, NEG)
        pm, alpha = softmax(sm, mm_ref, lm_ref)
        pv = None
        for i, m in enumerate(mb):
            t = _dot_nt(pm[:, i * PAGE:(i + 1) * PAGE], m[0:KV_LORA])
            pv = t if pv is None else pv + t
        accm_ref[...] = _rep(alpha, KV_LORA) * accm_ref[...] + pv

    update(kp, vp, mp, None, None)

    @pl.when(s_id == pl.num_programs(1) - 1)
    def _():
        key = lax.broadcasted_iota(jnp.int32, (1, PAGE), 1)
        tok_d = lax.broadcasted_iota(jnp.int32, (H_A * rows_h, 1), 0) % ts
        tok_m = lax.broadcasted_iota(jnp.int32, (H_B * ts, 1), 0) % ts
        update([ok_ref], [ov_ref], [om_ref], key <= tok_d, key <= tok_m)
        o = accd_ref[...] / ld_ref[...]
        lam = _lam(lam_ref, lam_init)
        for h in range(H_A):
            od = o[h * rows_h:h * rows_h + ts] - lam * o[h * rows_h + ts:(h + 1) * rows_h]
            od_ref[:, h * LANES:(h + 1) * LANES] = _rms(od, gsub_ref[...]) * (1.0 - lam_init)
        ol_ref[...] = accm_ref[...] / _rep(lm_ref[...], KV_LORA)


def _dec_even(pt, qd, ql, qr, own_k, own_v, own_m, lam_vec, gsub, ck, cv, cm, layer, npg, lam_init):
    nb, n_pages = pt.shape
    ts = qd.shape[1] // (H_A * 2)
    nsteps = n_pages // npg
    ptf = pt.reshape(-1)

    def page_spec(a, p):
        return pl.BlockSpec((None, None) + a.shape[2:],
                            lambda b, s, pt_ref: (layer, pt_ref[b * n_pages + s * npg + p], 0, 0))

    per_b = lambda a: pl.BlockSpec((None,) + a.shape[1:], lambda b, s, pt_ref: (b,) + (0,) * (a.ndim - 1))
    full = lambda a: pl.BlockSpec(a.shape, lambda b, s, pt_ref: (0,) * a.ndim)
    in_specs = ([per_b(qd), per_b(ql), per_b(qr), per_b(own_k), per_b(own_v), per_b(own_m),
                 full(lam_vec), full(gsub)]
                + [page_spec(ck, p) for p in range(npg)]
                + [page_spec(cv, p) for p in range(npg)]
                + [page_spec(cm, p) for p in range(npg)])
    rd, rm = qd.shape[1], ql.shape[1]
    grid_spec = pltpu.PrefetchScalarGridSpec(
        num_scalar_prefetch=1, grid=(nb, nsteps), in_specs=in_specs,
        out_specs=(pl.BlockSpec((None, ts, N_A), lambda b, s, pt_ref: (b, 0, 0)),
                   pl.BlockSpec((None, rm, KV_LORA), lambda b, s, pt_ref: (b, 0, 0))),
        scratch_shapes=[pltpu.VMEM((rd, LANES), F32), pltpu.VMEM((rd, LANES), F32), pltpu.VMEM((rd, LANES), F32),
                        pltpu.VMEM((rm, LANES), F32), pltpu.VMEM((rm, LANES), F32), pltpu.VMEM((rm, KV_LORA), F32)])
    return pl.pallas_call(
        functools.partial(_dec_even_kernel, npg=npg, ts=ts, lam_init=lam_init),
        grid_spec=grid_spec,
        out_shape=(jax.ShapeDtypeStruct((nb, ts, N_A), F32), jax.ShapeDtypeStruct((nb, rm, KV_LORA), F32)),
        compiler_params=_cparams(("parallel", "arbitrary")), name="dec_even",
    )(ptf, qd, ql, qr, own_k, own_v, own_m, lam_vec, gsub, *([ck] * npg), *([cv] * npg), *([cm] * npg))


def _uv_kernel(ol_ref, wuv_ref, o_ref):
    o_ref[...] = _dot(ol_ref[...].astype(BF16), wuv_ref[...]).astype(o_ref.dtype)


def _uv(ol, wuv):
    _, r, _ = ol.shape
    return pl.pallas_call(
        _uv_kernel, grid=(H_B,),
        in_specs=[pl.BlockSpec((None, r, KV_LORA), lambda h: (h, 0, 0)),
                  pl.BlockSpec((None, KV_LORA, V_B), lambda h: (h, 0, 0))],
        out_specs=pl.BlockSpec((r, V_B), lambda h: (0, h)),
        out_shape=jax.ShapeDtypeStruct((r, H_B * V_B), BF16),
        compiler_params=_cparams(("parallel",)), name="mla_uv",
    )(ol, wuv)


def _dec_sb_kernel(pt_ref, q_ref, u_ref, *rest, npg, ts, first):
    prev = rest[0:2]
    kp = rest[2:2 + npg]
    vp = rest[2 + npg:2 + 2 * npg]
    c_ref, acc_ref = rest[2 + 2 * npg:]
    s_id = pl.program_id(1)

    def block(k_ref, v_ref, mask):
        z = _dot(q_ref[...], k_ref[...].astype(BF16))
        _sb_step(z, v_ref[...].astype(BF16), u_ref[...], mask, c_ref, acc_ref, v_feature_major=True)

    @pl.when(s_id == 0)
    def _():
        if first:
            c_ref[...] = jnp.zeros(c_ref.shape, F32)
            acc_ref[...] = jnp.zeros(acc_ref.shape, F32)
            key = lax.broadcasted_iota(jnp.int32, (1, PAGE), 1)
            tok = lax.broadcasted_iota(jnp.int32, (H_C * ts, 1), 0) % ts
            block(prev[0], prev[1], key < tok)
        else:
            c_ref[...] = prev[0][...]
            acc_ref[...] = prev[1][...]

    @pl.when(_sb_live(c_ref))
    def _():
        for p in range(npg):
            @pl.when(_sb_live(c_ref))
            def _():
                block(kp[p], vp[p], None)


def _dec_sb(pt, q, prev, ck, cv, layer, npg, hi, nsteps, first):
    nb, n_pages = pt.shape
    ts = q.shape[1] // H_C
    ptf = pt.reshape(-1)
    w = ck.shape[-2]
    u = _tri(PAGE)

    def page_spec(p):
        return pl.BlockSpec((None, None, w, PAGE),
                            lambda b, s, pt_ref: (layer, pt_ref[b * n_pages + hi - 1 - (s * npg + p)], 0, 0))

    per_b = lambda a: pl.BlockSpec((None,) + a.shape[1:], lambda b, s, pt_ref: (b,) + (0,) * (a.ndim - 1))
    in_specs = ([per_b(q), pl.BlockSpec(u.shape, lambda b, s, pt_ref: (0, 0)), per_b(prev[0]), per_b(prev[1])]
                + [page_spec(p) for p in range(npg)] + [page_spec(p) for p in range(npg)])
    rq = q.shape[1]
    grid_spec = pltpu.PrefetchScalarGridSpec(
        num_scalar_prefetch=1, grid=(nb, nsteps), in_specs=in_specs,
        out_specs=(pl.BlockSpec((None, rq, LANES), lambda b, s, pt_ref: (b, 0, 0)),
                   pl.BlockSpec((None, rq, w), lambda b, s, pt_ref: (b, 0, 0))))
    return pl.pallas_call(
        functools.partial(_dec_sb_kernel, npg=npg, ts=ts, first=first),
        grid_spec=grid_spec,
        out_shape=(jax.ShapeDtypeStruct((nb, rq, LANES), F32), jax.ShapeDtypeStruct((nb, rq, w), F32)),
        compiler_params=_cparams(("parallel", "arbitrary")), name="dec_sb_first" if first else "dec_sb_rest",
    )(ptf, q, u, prev[0], prev[1], *([ck] * npg), *([cv] * npg))


def _dec_sb_all(pt, q, own_k, own_v, ck, cv, layer, npg):
    n_pages = pt.shape[1]
    state = _dec_sb(pt, q, (own_k, own_v), ck, cv, layer, npg, n_pages, 1, True)
    if n_pages > npg:
        rest = lambda st: _dec_sb(pt, q, st, ck, cv, layer, npg, n_pages - npg, n_pages // npg - 1, False)
        state = lax.cond(jnp.max(state[0]) > SB_DONE, rest, lambda st: st, state)
    return state[1]


def _pad_rows(a, n):
    return jnp.pad(a, ((0, 0), (0, n - a.shape[1]), (0, 0)))


def _feature_major(a, n):
    at = jnp.swapaxes(a, 1, 2)
    return jnp.pad(at, ((0, 0), (0, 0), (0, n - at.shape[2])))


def _block_diag_queries(q, groups, width):
    nb, ts, _ = q.shape
    eye = jnp.eye(groups, dtype=q.dtype)
    qg = q.reshape(nb, ts, groups, width)
    out = jnp.einsum('btgw,hg->bhtgw', qg, eye)
    return out.reshape(nb, groups * ts, groups * width)


def kernel(x_prompt, x_sample, cache_diff_k, cache_diff_v, cache_mla, cache_sb_k, cache_sb_v, page_table,
           g_mix, g_ffn, w_in_even, diff_lambda, g_diff_sub, g_mla_q, g_mla_kv, w_mla_uq, w_mla_uk, w_mla_uv,
           w_out_even, w_in_odd, w_out_odd, w_ff1, w_ff2, g_final):
    b, t, d = x_prompt.shape
    nb, ts, _ = x_sample.shape
    n_pages = page_table.shape[1]
    past = n_pages * PAGE
    depth = g_mix.shape[0]
    n_even, n_odd = (depth + 1) // 2, depth // 2
    n_pool = cache_diff_k.shape[1]
    npg = _tile(n_pages, 8)
    npg_e = _tile(n_pages, 16)

    xp = x_prompt.reshape(b * t, d)
    xs = x_sample.reshape(nb * ts, d)
    tm_p = _tile(b * t, 512)
    tm_s = _tile(nb * ts, 512)
    tab_p = _rope_tables(jnp.arange(t))
    tab_s = jnp.tile(_rope_tables(past + jnp.arange(ts)), (tm_s // ts, 1))

    ck = jnp.transpose(cache_diff_k, (0, 1, 3, 4, 5, 2)).reshape(n_even, n_pool, N_A, PAGE)
    cv = cache_diff_v.reshape(n_even, n_pool, PAGE * H_A, 2 * DH_A)
    cm = jnp.swapaxes(cache_mla, 2, 3)
    csk = jnp.transpose(cache_sb_k, (0, 1, 3, 4, 2)).reshape(n_odd, n_pool, NK_C, PAGE)
    csv = jnp.transpose(cache_sb_v, (0, 1, 3, 4, 2)).reshape(n_odd, n_pool, NK_C, PAGE)

    row2 = lambda v: v.reshape(1, -1)
    outs_p = {k: [] for k in ("dk", "dv", "ml", "sk", "sv")}
    outs_s = {k: [] for k in ("dk", "dv", "ml", "sk", "sv")}

    for l in range(depth):
        w1 = w_ff1[l].astype(BF16)
        w2 = w_ff2[l].astype(BF16)
        final = l == depth - 1
        gf = row2(g_final)
        if l % 2 == 0:
            e = l // 2
            lam_init = 0.8 - 0.6 * math.exp(-0.3 * l)
            wi = w_in_even[e]
            c0 = 3 * N_A + Q_LORA + KV_LORA
            w_all = jnp.concatenate([wi[:, :c0]] + [wi[:, c0:]] * H_B, axis=1).astype(BF16)
            uq = w_mla_uq[e].reshape(Q_LORA, H_B, NOPE_B + ROPE_B)
            wuq = jnp.concatenate([uq[:, :, :NOPE_B].reshape(Q_LORA, H_B * NOPE_B),
                                   uq[:, :, NOPE_B:].reshape(Q_LORA, H_B * ROPE_B)], axis=1).astype(BF16)
            wuk = jnp.einsum('hcn,hg->hngc', w_mla_uk[e], jnp.eye(H_B, dtype=F32)).reshape(
                H_B * NOPE_B, H_B * KV_LORA).astype(BF16)
            wuv = w_mla_uv[e].astype(BF16)
            wo = w_out_even[e].astype(BF16)
            gq, gkv, gsub = row2(g_mla_q[e]), row2(g_mla_kv[e]), row2(g_diff_sub[e])
            lam_vec = diff_lambda[e]
            gm = row2(g_mix[l])

            kd, vd, ml, qd_b, kd_b, vd_b, qm_b, km_b = _even_proj(xp, gm, w_all, tab_p, gq, gkv, wuq, wuk, tm_p)
            outs_p["dk"].append(kd.reshape(b, t, H_A, 2, DH_A))
            outs_p["dv"].append(vd.reshape(b, t, H_A, 2 * DH_A))
            outs_p["ml"].append(ml.reshape(b, t, KV_LORA + ROPE_B))
            od = _diff_attn(qd_b, kd_b, vd_b, lam_vec, gsub, b, t, lam_init)
            om = _mla_attn(qm_b, km_b, wuv, b, t)
            xp = _post(xp, od, 0, om, 0, wo, row2(g_ffn[l]), w1, w2, gf, final)

            kd, vd, ml, qd_b, _, _, qm_b, _ = _even_proj(xs, gm, w_all, tab_s, gq, gkv, wuq, wuk, tm_s)
            outs_s["dk"].append(kd.reshape(nb, ts, H_A, 2, DH_A))
            outs_s["dv"].append(vd.reshape(nb, ts, H_A, 2 * DH_A))
            outs_s["ml"].append(ml.reshape(nb, ts, KV_LORA + ROPE_B))
            qbd = _block_diag_queries(qd_b.reshape(nb, ts, N_A), H_A * 2, DH_A)
            qm4 = qm_b.reshape(H_B, nb, ts, MLA_W)
            ql = jnp.transpose(qm4[..., :KV_LORA], (1, 0, 2, 3)).reshape(nb, H_B * ts, KV_LORA)
            qr = jnp.stack([qm4[h, :, :, KV_LORA + h * ROPE_B:KV_LORA + (h + 1) * ROPE_B] for h in range(H_B)],
                           axis=1).reshape(nb, H_B * ts, ROPE_B)
            od_s, ol_s = _dec_even(page_table, qbd, ql, qr,
                                   _feature_major(kd.reshape(nb, ts, N_A), PAGE),
                                   _pad_rows(vd.reshape(nb, ts * H_A, 2 * DH_A), PAGE * H_A),
                                   _feature_major(ml.reshape(nb, ts, KV_LORA + ROPE_B), PAGE),
                                   lam_vec, gsub, ck, cv, cm, e, npg_e, lam_init)
            ol_h = jnp.transpose(ol_s.reshape(nb, H_B, ts, KV_LORA), (1, 0, 2, 3)).reshape(H_B, nb * ts, KV_LORA)
            om_s = _uv(ol_h, wuv)
            xs = _post(xs, od_s.reshape(nb * ts, N_A), 0, om_s, 0, wo, row2(g_ffn[l]), w1, w2, gf, final)
        else:
            o = l // 2
            wi = w_in_odd[o]
            wq, wk, wv = wi[:, :NQ_C], wi[:, NQ_C:NQ_C + NK_C], wi[:, NQ_C + NK_C:]
            dup = lambda w: jnp.repeat(w.reshape(d, KVH_C, 1, DH_C), 2, axis=2).reshape(d, 2 * NK_C)
            w_all = jnp.concatenate([wq, wk, wv, dup(wk), dup(wv)], axis=1).astype(BF16)
            wo = w_out_odd[o].astype(BF16)
            gm = row2(g_mix[l])

            k, v, q_b, k2_b, v2_b = _odd_proj(xp, gm, w_all, tm_p)
            outs_p["sk"].append(k.reshape(b, t, KVH_C, DH_C))
            outs_p["sv"].append(v.reshape(b, t, KVH_C, DH_C))
            att = _sb_attn(q_b, k2_b, v2_b, b, t)
            xp = _post(xp, att, 0, att, 1, wo, row2(g_ffn[l]), w1, w2, gf, final)

            k, v, q_b, _, _ = _odd_proj(xs, gm, w_all, tm_s)
            outs_s["sk"].append(k.reshape(nb, ts, KVH_C, DH_C))
            outs_s["sv"].append(v.reshape(nb, ts, KVH_C, DH_C))
            qg = jnp.transpose(q_b.reshape(nb, ts, KVH_C, G_C, DH_C), (0, 2, 3, 1, 4))
            qbd = jnp.einsum('bkgtd,kj->bkgtjd', qg, jnp.eye(KVH_C, dtype=BF16)).reshape(
                nb, H_C * ts, NK_C)
            acc_s = _dec_sb_all(page_table, qbd, _feature_major(k.reshape(nb, ts, NK_C), PAGE),
                                _feature_major(v.reshape(nb, ts, NK_C), PAGE), csk, csv, o, npg)
            acc5 = acc_s.reshape(nb, KVH_C, G_C, ts, KVH_C, DH_C)
            att_s = jnp.stack([acc5[:, kh, :, :, kh, :] for kh in range(KVH_C)], axis=1)
            att_s = jnp.transpose(att_s, (0, 3, 1, 2, 4)).reshape(nb * ts, NQ_C)
            xs = _post(xs, att_s, 0, att_s, 1, wo, row2(g_ffn[l]), w1, w2, gf, final)

    st = lambda xs_: jnp.stack(xs_)
    return (xp.reshape(b, t, d), xs.reshape(nb, ts, d),
            st(outs_p["dk"]), st(outs_p["dv"]), st(outs_p["ml"]), st(outs_p["sk"]), st(outs_p["sv"]),
            st(outs_s["dk"]), st(outs_s["dv"]), st(outs_s["ml"]), st(outs_s["sk"]), st(outs_s["sv"]))
```

```python
import functools
import math

import jax
import jax.numpy as jnp
from jax import lax
from jax.experimental import pallas as pl
from jax.experimental.pallas import tpu as pltpu

F32 = jnp.float32
BF16 = jnp.bfloat16

EPS = 1e-6
NEG = -1e30
H_A = 4
DH_A = 64
ROT_A = DH_A // 4
ROPE_THETA = 500000.0
SCALE_A = DH_A ** -0.5
H_B = 4
Q_LORA = 384
KV_LORA = 256
NOPE_B = 64
ROPE_B = 32
V_B = 128
MLA_THETA = 10000.0
SCALE_B = (NOPE_B + ROPE_B) ** -0.5
H_C = 16
KVH_C = 4
G_C = H_C // KVH_C
DH_C = 64
SCALE_C = DH_C ** -0.5
PAGE = 128
LOG2E = math.log2(math.e)
QSCALE_A = SCALE_A * LOG2E
QSCALE_B = SCALE_B * LOG2E
QSCALE_C = SCALE_C * LOG2E

LANES = 128
VMEM_LIMIT = 48 * 1024 * 1024

N_A = H_A * 2 * DH_A
MLA_W = KV_LORA + LANES


def _tile(n, pref):
    t = min(n, pref)
    while n % t:
        t //= 2
    return t


def _dot(a, b):
    return jnp.dot(a, b, preferred_element_type=F32)


def _dot_nt(a, b):
    return lax.dot_general(a, b, (((1,), (1,)), ((), ())), preferred_element_type=F32)


def _rep(x, width):
    n = width // LANES
    return x if n == 1 else pltpu.repeat(x, n, 1)


def _store_row_groups(o_ref, y, width):
    tm = y.shape[0]
    g = y.shape[1] // width
    for j in range(g):
        o_ref[pl.ds(j, tm, stride=g), :] = y[:, j * width:(j + 1) * width]


def _rms(x, g):
    return x * lax.rsqrt(jnp.mean(x * x, axis=-1, keepdims=True) + EPS) * g


def _cparams(sem, vmem=VMEM_LIMIT):
    return pltpu.CompilerParams(dimension_semantics=sem, vmem_limit_bytes=vmem)


def _rope_tables(pos):
    posf = pos.astype(F32)[:, None]
    lane = jnp.arange(LANES)

    def tab(period, half, theta):
        inv = theta ** (-jnp.arange(half, dtype=F32) / half)
        ang = posf * inv[None, :]
        cos, sin = jnp.cos(ang), jnp.sin(ang)
        d = lane % period
        first = d < half
        second = (d >= half) & (d < 2 * half)
        idx = jnp.where(first, d, jnp.where(second, d - half, 0))
        c = jnp.where((first | second)[None, :], cos[:, idx], 1.0)
        s1 = jnp.where(first[None, :], -sin[:, idx], 0.0)
        s2 = jnp.where(second[None, :], sin[:, idx], 0.0)
        return [c, s1, s2]

    return jnp.concatenate(tab(DH_A, ROT_A // 2, ROPE_THETA) + tab(ROPE_B, ROPE_B // 2, MLA_THETA), axis=1)


def _rope(y, c, s1, s2, half):
    outs = []
    for k in range(y.shape[1] // LANES):
        yb = y[:, k * LANES:(k + 1) * LANES]
        outs.append(yb * c + pltpu.roll(yb, LANES - half, 1) * s1 + pltpu.roll(yb, half, 1) * s2)
    return outs[0] if len(outs) == 1 else jnp.concatenate(outs, axis=1)


def _even_proj_kernel(x_ref, g_ref, w_ref, tab_ref, gq_ref, gkv_ref, wuq_ref, wuk_ref,
                      kd_o, vd_o, mla_o, qd_b, kd_b, vd_b, qm_b, km_b):
    hb = _rms(x_ref[...], g_ref[...]).astype(BF16)
    tab = tab_ref[...]
    ca, s1a, s2a, cb, s1b, s2b = [tab[:, k * LANES:(k + 1) * LANES] for k in range(6)]
    o = 0
    qd = _rope(_dot(hb, w_ref[:, o:o + N_A]), ca, s1a, s2a, ROT_A // 2)
    qd_b[...] = (qd * QSCALE_A).astype(BF16)
    o += N_A
    kd = _rope(_dot(hb, w_ref[:, o:o + N_A]), ca, s1a, s2a, ROT_A // 2)
    _store_row_groups(kd_o, kd, DH_A)
    kd_b[...] = kd.astype(BF16)
    o += N_A
    vd = _dot(hb, w_ref[:, o:o + N_A])
    _store_row_groups(vd_o, vd, 2 * DH_A)
    vd_b[...] = vd.astype(BF16)
    o += N_A
    cq = _dot(hb, w_ref[:, o:o + Q_LORA])
    o += Q_LORA
    ckv = _dot(hb, w_ref[:, o:o + KV_LORA])
    o += KV_LORA
    kr4 = _dot(hb, w_ref[:, o:o + LANES])
    cqn = _rms(cq, gq_ref[...]).astype(BF16)
    qn = _dot(cqn, wuq_ref[:, 0:H_B * NOPE_B]).astype(BF16)
    qr = _dot(cqn, wuq_ref[:, H_B * NOPE_B:])
    qlat = _dot(qn, wuk_ref[...])
    qr = _rope(qr, cb, s1b, s2b, ROPE_B // 2)
    ckvn = _rms(ckv, gkv_ref[...])
    kr4 = _rope(kr4, cb, s1b, s2b, ROPE_B // 2)
    mla_o[:, 0:KV_LORA] = ckvn
    mla_o[:, KV_LORA:KV_LORA + ROPE_B] = kr4[:, 0:ROPE_B]
    km_b[:, 0:KV_LORA] = ckvn.astype(BF16)
    km_b[:, KV_LORA:] = kr4.astype(BF16)
    lane = lax.broadcasted_iota(jnp.int32, (1, LANES), 1)
    for h in range(H_B):
        qm_b[h, :, 0:KV_LORA] = (qlat[:, h * KV_LORA:(h + 1) * KV_LORA] * QSCALE_B).astype(BF16)
        qm_b[h, :, KV_LORA:] = jnp.where((lane >= h * ROPE_B) & (lane < (h + 1) * ROPE_B), qr * QSCALE_B, 0.0).astype(BF16)


def _even_proj(x, g, w, tab, gq, gkv, wuq, wuk, tm):
    r, d = x.shape
    nt = tab.shape[0] // tm
    wcols = w.shape[1]
    row = lambda n: pl.BlockSpec((tm, n), lambda i: (i, 0))
    full = lambda a: pl.BlockSpec(a.shape, lambda i: (0,) * a.ndim)
    out_shape = (
        jax.ShapeDtypeStruct((r * H_A * 2, DH_A), F32), jax.ShapeDtypeStruct((r * H_A, 2 * DH_A), F32),
        jax.ShapeDtypeStruct((r, KV_LORA + ROPE_B), F32),
        jax.ShapeDtypeStruct((r, N_A), BF16), jax.ShapeDtypeStruct((r, N_A), BF16),
        jax.ShapeDtypeStruct((r, N_A), BF16),
        jax.ShapeDtypeStruct((H_B, r, MLA_W), BF16), jax.ShapeDtypeStruct((r, MLA_W), BF16))
    out_specs = (pl.BlockSpec((tm * H_A * 2, DH_A), lambda i: (i, 0)),
                 pl.BlockSpec((tm * H_A, 2 * DH_A), lambda i: (i, 0)),
                 row(KV_LORA + ROPE_B), row(N_A), row(N_A), row(N_A),
                 pl.BlockSpec((H_B, tm, MLA_W), lambda i: (0, i, 0)), row(MLA_W))
    return pl.pallas_call(
        _even_proj_kernel, grid=(r // tm,),
        in_specs=[row(d), full(g), full(w), pl.BlockSpec((tm, tab.shape[1]), lambda i: (i % nt, 0)),
                  full(gq), full(gkv), full(wuq), full(wuk)],
        out_specs=out_specs, out_shape=out_shape,
        compiler_params=_cparams(("parallel",)), name="even_proj",
    )(x, g, w, tab, gq, gkv, wuq, wuk)


def _lam(lam_ref, lam_init):
    lf = lam_ref[...]
    a = jnp.sum(lf[0:1] * lf[1:2], axis=-1, keepdims=True)
    b = jnp.sum(lf[2:3] * lf[3:4], axis=-1, keepdims=True)
    return jnp.exp(a) - jnp.exp(b) + lam_init


def _softmax_step(s, vb, m_ref, l_ref, acc_ref):
    m_prev = m_ref[...]
    m_new = jnp.maximum(m_prev, jnp.max(s, axis=-1, keepdims=True))
    alpha = jnp.exp2(m_prev - m_new)
    p = jnp.exp2(s - _rep(m_new, s.shape[1]))
    l_ref[...] = alpha * l_ref[...] + jnp.sum(p, axis=-1, keepdims=True)
    pv = _dot(p.astype(BF16), vb)
    acc_ref[...] = _rep(alpha, acc_ref.shape[1]) * acc_ref[...] + pv
    m_ref[...] = m_new


def _attn_block(qq_ref, kb, vb, rows, cols, m_ref, l_ref, acc_ref, nsplit):
    n = qq_ref.shape[0] // nsplit
    for r in range(nsplit):
        sl = slice(r * n, (r + 1) * n)
        s = _dot_nt(qq_ref[sl, :], kb)
        if cols is not None:
            s = jnp.where(cols <= rows[sl], s, NEG)
        _softmax_step(s, vb, m_ref.at[sl, :], l_ref.at[sl, :], acc_ref.at[sl, :])


def _diff_attn_kernel(q_ref, k_ref, v_ref, lam_ref, gsub_ref, o_ref, qq_ref, m_ref, l_ref, acc_ref,
                      *, tq, tk, lam_init, nsplit):
    i = pl.program_id(2)
    lane = lax.broadcasted_iota(jnp.int32, (1, LANES), 1)
    q = q_ref[...]
    zero = jnp.zeros_like(q)
    qq_ref[0:tq, :] = jnp.where(lane < DH_A, q, zero)
    qq_ref[tq:2 * tq, :] = jnp.where(lane >= DH_A, q, zero)
    m_ref[...] = jnp.full(m_ref.shape, NEG, F32)
    l_ref[...] = jnp.zeros(l_ref.shape, F32)
    acc_ref[...] = jnp.zeros(acc_ref.shape, F32)

    rows = lax.broadcasted_iota(jnp.int32, (2 * tq, 1), 0) % tq

    def block(off, cols):
        _attn_block(qq_ref, k_ref[pl.ds(off, tk), :], v_ref[pl.ds(off, tk), :], rows, cols,
                    m_ref, l_ref, acc_ref, nsplit)

    def body(j, carry):
        block(pl.multiple_of(j * tk, tk), None)
        return carry

    lax.fori_loop(0, i * (tq // tk), body, 0)
    cols = lax.broadcasted_iota(jnp.int32, (1, tk), 1)
    for d in range(tq // tk):
        block(pl.multiple_of(i * tq + d * tk, tk), cols + d * tk)

    o = acc_ref[...] / l_ref[...]
    od = o[0:tq] - _lam(lam_ref, lam_init) * o[tq:2 * tq]
    o_ref[...] = (_rms(od, gsub_ref[...]) * (1.0 - lam_init)).astype(o_ref.dtype)


def _diff_attn(qd, kd, vd, lam_vec, gsub, b, t, lam_init):
    tq = _tile(t, 1024)
    tk = _tile(tq, 1024)
    nq = t // tq
    return pl.pallas_call(
        functools.partial(_diff_attn_kernel, tq=tq, tk=tk, lam_init=lam_init, nsplit=2),
        grid=(b, H_A, nq),
        in_specs=[pl.BlockSpec((tq, LANES), lambda bb, h, i: (bb * nq + i, h)),
                  pl.BlockSpec((t, LANES), lambda bb, h, i: (bb, h)),
                  pl.BlockSpec((t, LANES), lambda bb, h, i: (bb, h)),
                  pl.BlockSpec(lam_vec.shape, lambda bb, h, i: (0, 0)),
                  pl.BlockSpec(gsub.shape, lambda bb, h, i: (0, 0))],
        out_specs=pl.BlockSpec((tq, LANES), lambda bb, h, i: (bb * nq + i, h)),
        out_shape=jax.ShapeDtypeStruct((b * t, N_A), BF16),
        scratch_shapes=[pltpu.VMEM((2 * tq, LANES), BF16), pltpu.VMEM((2 * tq, LANES), F32),
                        pltpu.VMEM((2 * tq, LANES), F32), pltpu.VMEM((2 * tq, LANES), F32)],
        compiler_params=_cparams(("parallel", "parallel", "arbitrary")), name="diff_attn",
    )(qd, kd, vd, lam_vec, gsub)


def _mla_attn_kernel(q_ref, k_ref, wuv_ref, o_ref, qq_ref, m_ref, l_ref, acc_ref, *, tq, tk, nsplit):
    i = pl.program_id(1)
    for h in range(H_B):
        qq_ref[h * tq:(h + 1) * tq, :] = q_ref[h]
    m_ref[...] = jnp.full(m_ref.shape, NEG, F32)
    l_ref[...] = jnp.zeros(l_ref.shape, F32)
    acc_ref[...] = jnp.zeros(acc_ref.shape, F32)

    rows = lax.broadcasted_iota(jnp.int32, (H_B * tq, 1), 0) % tq

    def block(off, cols):
        kb = k_ref[pl.ds(off, tk), :]
        _attn_block(qq_ref, kb, kb[:, 0:KV_LORA], rows, cols, m_ref, l_ref, acc_ref, nsplit)

    def body(j, carry):
        block(pl.multiple_of(j * tk, tk), None)
        return carry

    lax.fori_loop(0, i * (tq // tk), body, 0)
    cols = lax.broadcasted_iota(jnp.int32, (1, tk), 1)
    for d in range(tq // tk):
        block(pl.multiple_of(i * tq + d * tk, tk), cols + d * tk)

    ol = (acc_ref[...] / _rep(l_ref[...], KV_LORA)).astype(BF16)
    for h in range(H_B):
        o_ref[:, h * V_B:(h + 1) * V_B] = _dot(ol[h * tq:(h + 1) * tq], wuv_ref[h]).astype(o_ref.dtype)


def _mla_attn(qm, km, wuv, b, t):
    tq = _tile(t, 512)
    tk = _tile(tq, 512)
    nq = t // tq
    return pl.pallas_call(
        functools.partial(_mla_attn_kernel, tq=tq, tk=tk, nsplit=2),
        grid=(b, nq),
        in_specs=[pl.BlockSpec((H_B, tq, MLA_W), lambda bb, i: (0, bb * nq + i, 0)),
                  pl.BlockSpec((t, MLA_W), lambda bb, i: (bb, 0)),
                  pl.BlockSpec(wuv.shape, lambda bb, i: (0, 0, 0))],
        out_specs=pl.BlockSpec((tq, H_B * V_B), lambda bb, i: (bb * nq + i, 0)),
        out_shape=jax.ShapeDtypeStruct((b * t, H_B * V_B), BF16),
        scratch_shapes=[pltpu.VMEM((H_B * tq, MLA_W), BF16), pltpu.VMEM((H_B * tq, LANES), F32),
                        pltpu.VMEM((H_B * tq, LANES), F32), pltpu.VMEM((H_B * tq, KV_LORA), F32)],
        compiler_params=_cparams(("parallel", "arbitrary")), name="mla_attn",
    )(qm, km, wuv)


def _post_kernel(x_ref, a1_ref, a2_ref, wo_ref, g_ref, w1_ref, w2_ref, gf_ref, o_ref,
                 x1_ref, hn_ref, acc_ref, *, final):
    j = pl.program_id(1)
    half = a1_ref.shape[1]

    @pl.when(j == 0)
    def _():
        mix = (_dot(a1_ref[...].astype(BF16), wo_ref[0:half, :])
               + _dot(a2_ref[...].astype(BF16), wo_ref[half:2 * half, :]))
        x1 = x_ref[...] + mix
        x1_ref[...] = x1
        hn_ref[...] = _rms(x1, g_ref[...]).astype(BF16)
        acc_ref[...] = jnp.zeros(acc_ref.shape, F32)

    h1 = jnp.maximum(_dot(hn_ref[...], w1_ref[...]), 0.0)
    acc_ref[...] += _dot((h1 * h1).astype(BF16), w2_ref[...])

    @pl.when(j == pl.num_programs(1) - 1)
    def _():
        y = x1_ref[...] + acc_ref[...]
        if final:
            y = _rms(y, gf_ref[...])
        o_ref[...] = y


def _post(x, a1, c1, a2, c2, wo, g, w1, w2, gf, final):
    r, d = x.shape
    tm = _tile(r, 512)
    dff = w1.shape[1]
    tf = _tile(dff, 1024)
    half = d // 2
    return pl.pallas_call(
        functools.partial(_post_kernel, final=final),
        grid=(r // tm, dff // tf),
        in_specs=[pl.BlockSpec((tm, d), lambda i, j: (i, 0)),
                  pl.BlockSpec((tm, half), lambda i, j: (i, c1)),
                  pl.BlockSpec((tm, half), lambda i, j: (i, c2)),
                  pl.BlockSpec(wo.shape, lambda i, j: (0, 0)),
                  pl.BlockSpec(g.shape, lambda i, j: (0, 0)),
                  pl.BlockSpec((d, tf), lambda i, j: (0, j)),
                  pl.BlockSpec((tf, d), lambda i, j: (j, 0)),
                  pl.BlockSpec(gf.shape, lambda i, j: (0, 0))],
        out_specs=pl.BlockSpec((tm, d), lambda i, j: (i, 0)),
        out_shape=jax.ShapeDtypeStruct((r, d), F32),
        scratch_shapes=[pltpu.VMEM((tm, d), F32), pltpu.VMEM((tm, d), BF16), pltpu.VMEM((tm, d), F32)],
        compiler_params=_cparams(("parallel", "arbitrary")), name="post_final" if final else "post",
    )(x, a1, a2, wo, g, w1, w2, gf)


NQ_C = H_C * DH_C
NK_C = KVH_C * DH_C


def _odd_proj_kernel(x_ref, g_ref, w_ref, k_o, v_o, q_b, k2_b, v2_b):
    hb = _rms(x_ref[...], g_ref[...]).astype(BF16)
    o = 0
    q_b[...] = (_dot(hb, w_ref[:, o:o + NQ_C]) * QSCALE_C).astype(BF16)
    o += NQ_C
    _store_row_groups(k_o, _dot(hb, w_ref[:, o:o + NK_C]), DH_C)
    o += NK_C
    _store_row_groups(v_o, _dot(hb, w_ref[:, o:o + NK_C]), DH_C)
    o += NK_C
    k2_b[...] = _dot(hb, w_ref[:, o:o + 2 * NK_C]).astype(BF16)
    o += 2 * NK_C
    v2_b[...] = _dot(hb, w_ref[:, o:o + 2 * NK_C]).astype(BF16)


def _odd_proj(x, g, w, tm):
    r, d = x.shape
    row = lambda n: pl.BlockSpec((tm, n), lambda i: (i, 0))
    full = lambda a: pl.BlockSpec(a.shape, lambda i: (0,) * a.ndim)
    return pl.pallas_call(
        _odd_proj_kernel, grid=(r // tm,),
        in_specs=[row(d), full(g), full(w)],
        out_specs=(pl.BlockSpec((tm * KVH_C, DH_C), lambda i: (i, 0)),
                   pl.BlockSpec((tm * KVH_C, DH_C), lambda i: (i, 0)),
                   row(NQ_C), row(2 * NK_C), row(2 * NK_C)),
        out_shape=(jax.ShapeDtypeStruct((r * KVH_C, DH_C), F32), jax.ShapeDtypeStruct((r * KVH_C, DH_C), F32),
                   jax.ShapeDtypeStruct((r, NQ_C), BF16), jax.ShapeDtypeStruct((r, 2 * NK_C), BF16),
                   jax.ShapeDtypeStruct((r, 2 * NK_C), BF16)),
        compiler_params=_cparams(("parallel",)), name="odd_proj",
    )(x, g, w)


SB_DONE = -150.0


def _sb_live(c_ref):
    return jnp.max(c_ref[...]) > SB_DONE


def _sb_step(z, vb, u, mask, c_ref, acc_ref, v_feature_major=False):
    tk = z.shape[1]
    ls = jnp.minimum(z, 0.0) - jnp.log2(1.0 + jnp.exp2(-jnp.abs(z)))
    lneg = ls - z
    if mask is not None:
        lneg = jnp.where(mask, lneg, 0.0)
    hi = lneg.astype(BF16)
    lo = (lneg - hi.astype(F32)).astype(BF16)
    suf = _dot(hi, u) + _dot(lo, u)
    c = c_ref[...]
    a = jnp.exp2(ls + suf + _rep(c, tk))
    if mask is not None:
        a = jnp.where(mask, a, 0.0)
    ab = a.astype(BF16)
    acc_ref[...] += _dot_nt(ab, vb) if v_feature_major else _dot(ab, vb)
    c_ref[...] = c + (suf[:, 0:1] + lneg[:, 0:1])


def _sb_attn_kernel(q_ref, k_ref, v_ref, u_ref, o_ref, qq_ref, c_ref, acc_ref, *, tq, tk, nsplit):
    i = pl.program_id(2)
    lane = lax.broadcasted_iota(jnp.int32, (1, LANES), 1)
    for g in range(G_C):
        qb = q_ref[:, (g // 2) * LANES:(g // 2 + 1) * LANES]
        keep = (lane < DH_C) if g % 2 == 0 else (lane >= DH_C)
        qq_ref[g * tq:(g + 1) * tq, :] = jnp.where(keep, qb, jnp.zeros_like(qb))
    c_ref[...] = jnp.zeros(c_ref.shape, F32)
    acc_ref[...] = jnp.zeros(acc_ref.shape, F32)

    rows = lax.broadcasted_iota(jnp.int32, (G_C * tq, 1), 0) % tq
    n = G_C * tq // nsplit

    def block(off, cols):
        kb = k_ref[pl.ds(off, tk), :]
        vb = v_ref[pl.ds(off, tk), :]
        for r in range(nsplit):
            sl = slice(r * n, (r + 1) * n)
            z = _dot_nt(qq_ref[sl, :], kb)
            mask = None if cols is None else cols < rows[sl]
            _sb_step(z, vb, u_ref[...], mask, c_ref.at[sl, :], acc_ref.at[sl, :])

    cols = lax.broadcasted_iota(jnp.int32, (1, tk), 1)
    for d in reversed(range(tq // tk)):
        block(pl.multiple_of(i * tq + d * tk, tk), cols + d * tk)

    def cond(carry):
        j, live = carry
        return jnp.logical_and(j >= 0, live)

    def body(carry):
        j, _ = carry
        block(pl.multiple_of(j * tk, tk), None)
        return j - 1, _sb_live(c_ref)

    lax.while_loop(cond, body, (i * (tq // tk) - 1, _sb_live(c_ref)))
    acc = acc_ref[...]
    for c2 in range(G_C // 2):
        o_ref[:, c2 * LANES:(c2 + 1) * LANES] = jnp.where(
            lane < DH_C, acc[(2 * c2) * tq:(2 * c2 + 1) * tq], acc[(2 * c2 + 1) * tq:(2 * c2 + 2) * tq]
        ).astype(o_ref.dtype)


def _tri(tk):
    j = jnp.arange(tk)
    return (j[:, None] > j[None, :]).astype(BF16)


def _sb_attn(q, k2, v2, b, t):
    tq = _tile(t, 256)
    tk = _tile(tq, 256)
    nq = t // tq
    u = _tri(tk)
    return pl.pallas_call(
        functools.partial(_sb_attn_kernel, tq=tq, tk=tk, nsplit=1),
        grid=(b, KVH_C, nq),
        in_specs=[pl.BlockSpec((tq, G_C * DH_C), lambda bb, h, i: (bb * nq + i, h)),
                  pl.BlockSpec((t, LANES), lambda bb, h, i: (bb, h)),
                  pl.BlockSpec((t, LANES), lambda bb, h, i: (bb, h)),
                  pl.BlockSpec(u.shape, lambda bb, h, i: (0, 0))],
        out_specs=pl.BlockSpec((tq, G_C * DH_C), lambda bb, h, i: (bb * nq + i, h)),
        out_shape=jax.ShapeDtypeStruct((b * t, NQ_C), BF16),
        scratch_shapes=[pltpu.VMEM((G_C * tq, LANES), BF16), pltpu.VMEM((G_C * tq, LANES), F32),
                        pltpu.VMEM((G_C * tq, LANES), F32)],
        compiler_params=_cparams(("parallel", "parallel", "arbitrary")), name="sb_attn",
    )(q, k2, v2, u)


def _dec_even_kernel(pt_ref, qd_ref, ql_ref, qr_ref, ok_ref, ov_ref, om_ref, lam_ref, gsub_ref, *rest,
                     npg, ts, lam_init):
    kp = rest[0:npg]
    vp = rest[npg:2 * npg]
    mp = rest[2 * npg:3 * npg]
    od_ref, ol_ref = rest[3 * npg:3 * npg + 2]
    md_ref, ld_ref, accd_ref, mm_ref, lm_ref, accm_ref = rest[3 * npg + 2:]
    s_id = pl.program_id(1)
    rows_h = 2 * ts

    @pl.when(s_id == 0)
    def _():
        md_ref[...] = jnp.full(md_ref.shape, NEG, F32)
        ld_ref[...] = jnp.zeros(ld_ref.shape, F32)
        accd_ref[...] = jnp.zeros(accd_ref.shape, F32)
        mm_ref[...] = jnp.full(mm_ref.shape, NEG, F32)
        lm_ref[...] = jnp.zeros(lm_ref.shape, F32)
        accm_ref[...] = jnp.zeros(accm_ref.shape, F32)

    def softmax(s, m_ref, l_ref):
        m_prev = m_ref[...]
        m_new = jnp.maximum(m_prev, jnp.max(s, axis=-1, keepdims=True))
        alpha = jnp.exp2(m_prev - m_new)
        p = jnp.exp2(s - _rep(m_new, s.shape[1]))
        l_ref[...] = alpha * l_ref[...] + jnp.sum(p, axis=-1, keepdims=True)
        m_ref[...] = m_new
        return p.astype(BF16), alpha

    def cat(xs, axis):
        return xs[0] if len(xs) == 1 else jnp.concatenate(xs, axis=axis)

    def update(ks, vs, ms, dmask, mmask):
        qd = qd_ref[...]
        sd = cat([_dot(qd, k[...].astype(BF16)) for k in ks], 1)
        if dmask is not None:
            sd = jnp.where(dmask, sd, NEG)
        pd, alpha = softmax(sd, md_ref, ld_ref)
        for h in range(H_A):
            vh = cat([v[pl.ds(h, PAGE, stride=H_A), :].astype(BF16) for v in vs], 0)
            r = slice(h * rows_h, (h + 1) * rows_h)
            accd_ref[r, :] = alpha[r] * accd_ref[r, :] + _dot(pd[r], vh)
        ql = ql_ref[...]
        qr = qr_ref[...]
        mb = [m[...].astype(BF16) for m in ms]
        sm = cat([_dot(ql, m[0:KV_LORA]) + _dot(qr, m[KV_LORA:KV_LORA + ROPE_B]) for m in mb], 1)
        if mmask is not None:
            sm = jnp.where(mmask, sm, NEG)
        pm, alpha = softmax(sm, mm_ref, lm_ref)
        pv = None
        for i, m in enumerate(mb):
            t = _dot_nt(pm[:, i * PAGE:(i + 1) * PAGE], m[0:KV_LORA])
            pv = t if pv is None else pv + t
        accm_ref[...] = _rep(alpha, KV_LORA) * accm_ref[...] + pv

    update(kp, vp, mp, None, None)

    @pl.when(s_id == pl.num_programs(1) - 1)
    def _():
        key = lax.broadcasted_iota(jnp.int32, (1, PAGE), 1)
        tok_d = lax.broadcasted_iota(jnp.int32, (H_A * rows_h, 1), 0) % ts
        tok_m = lax.broadcasted_iota(jnp.int32, (H_B * ts, 1), 0) % ts
        update([ok_ref], [ov_ref], [om_ref], key <= tok_d, key <= tok_m)
        o = accd_ref[...] / ld_ref[...]
        lam = _lam(lam_ref, lam_init)
        for h in range(H_A):
            od = o[h * rows_h:h * rows_h + ts] - lam * o[h * rows_h + ts:(h + 1) * rows_h]
            od_ref[:, h * LANES:(h + 1) * LANES] = _rms(od, gsub_ref[...]) * (1.0 - lam_init)
        ol_ref[...] = accm_ref[...] / _rep(lm_ref[...], KV_LORA)


def _dec_even(pt, qd, ql, qr, own_k, own_v, own_m, lam_vec, gsub, ck, cv, cm, layer, npg, lam_init):
    nb, n_pages = pt.shape
    ts = qd.shape[1] // (H_A * 2)
    nsteps = n_pages // npg
    ptf = pt.reshape(-1)

    def page_spec(a, p):
        return pl.BlockSpec((None, None) + a.shape[2:],
                            lambda b, s, pt_ref: (layer, pt_ref[b * n_pages + s * npg + p], 0, 0))

    per_b = lambda a: pl.BlockSpec((None,) + a.shape[1:], lambda b, s, pt_ref: (b,) + (0,) * (a.ndim - 1))
    full = lambda a: pl.BlockSpec(a.shape, lambda b, s, pt_ref: (0,) * a.ndim)
    in_specs = ([per_b(qd), per_b(ql), per_b(qr), per_b(own_k), per_b(own_v), per_b(own_m),
                 full(lam_vec), full(gsub)]
                + [page_spec(ck, p) for p in range(npg)]
                + [page_spec(cv, p) for p in range(npg)]
                + [page_spec(cm, p) for p in range(npg)])
    rd, rm = qd.shape[1], ql.shape[1]
    grid_spec = pltpu.PrefetchScalarGridSpec(
        num_scalar_prefetch=1, grid=(nb, nsteps), in_specs=in_specs,
        out_specs=(pl.BlockSpec((None, ts, N_A), lambda b, s, pt_ref: (b, 0, 0)),
                   pl.BlockSpec((None, rm, KV_LORA), lambda b, s, pt_ref: (b, 0, 0))),
        scratch_shapes=[pltpu.VMEM((rd, LANES), F32), pltpu.VMEM((rd, LANES), F32), pltpu.VMEM((rd, LANES), F32),
                        pltpu.VMEM((rm, LANES), F32), pltpu.VMEM((rm, LANES), F32), pltpu.VMEM((rm, KV_LORA), F32)])
    return pl.pallas_call(
        functools.partial(_dec_even_kernel, npg=npg, ts=ts, lam_init=lam_init),
        grid_spec=grid_spec,
        out_shape=(jax.ShapeDtypeStruct((nb, ts, N_A), F32), jax.ShapeDtypeStruct((nb, rm, KV_LORA), F32)),
        compiler_params=_cparams(("parallel", "arbitrary")), name="dec_even",
    )(ptf, qd, ql, qr, own_k, own_v, own_m, lam_vec, gsub, *([ck] * npg), *([cv] * npg), *([cm] * npg))


def _uv_kernel(ol_ref, wuv_ref, o_ref):
    o_ref[...] = _dot(ol_ref[...].astype(BF16), wuv_ref[...]).astype(o_ref.dtype)


def _uv(ol, wuv):
    _, r, _ = ol.shape
    return pl.pallas_call(
        _uv_kernel, grid=(H_B,),
        in_specs=[pl.BlockSpec((None, r, KV_LORA), lambda h: (h, 0, 0)),
                  pl.BlockSpec((None, KV_LORA, V_B), lambda h: (h, 0, 0))],
        out_specs=pl.BlockSpec((r, V_B), lambda h: (0, h)),
        out_shape=jax.ShapeDtypeStruct((r, H_B * V_B), BF16),
        compiler_params=_cparams(("parallel",)), name="mla_uv",
    )(ol, wuv)


def _dec_sb_kernel(pt_ref, q_ref, u_ref, *rest, npg, ts, first):
    prev = rest[0:2]
    kp = rest[2:2 + npg]
    vp = rest[2 + npg:2 + 2 * npg]
    c_ref, acc_ref = rest[2 + 2 * npg:]
    s_id = pl.program_id(1)

    def block(k_ref, v_ref, mask):
        z = _dot(q_ref[...], k_ref[...].astype(BF16))
        _sb_step(z, v_ref[...].astype(BF16), u_ref[...], mask, c_ref, acc_ref, v_feature_major=True)

    @pl.when(s_id == 0)
    def _():
        if first:
            c_ref[...] = jnp.zeros(c_ref.shape, F32)
            acc_ref[...] = jnp.zeros(acc_ref.shape, F32)
            key = lax.broadcasted_iota(jnp.int32, (1, PAGE), 1)
            tok = lax.broadcasted_iota(jnp.int32, (H_C * ts, 1), 0) % ts
            block(prev[0], prev[1], key < tok)
        else:
            c_ref[...] = prev[0][...]
            acc_ref[...] = prev[1][...]

    @pl.when(_sb_live(c_ref))
    def _():
        for p in range(npg):
            @pl.when(_sb_live(c_ref))
            def _():
                block(kp[p], vp[p], None)


def _dec_sb(pt, q, prev, ck, cv, layer, npg, hi, nsteps, first):
    nb, n_pages = pt.shape
    ts = q.shape[1] // H_C
    ptf = pt.reshape(-1)
    w = ck.shape[-2]
    u = _tri(PAGE)

    def page_spec(p):
        return pl.BlockSpec((None, None, w, PAGE),
                            lambda b, s, pt_ref: (layer, pt_ref[b * n_pages + hi - 1 - (s * npg + p)], 0, 0))

    per_b = lambda a: pl.BlockSpec((None,) + a.shape[1:], lambda b, s, pt_ref: (b,) + (0,) * (a.ndim - 1))
    in_specs = ([per_b(q), pl.BlockSpec(u.shape, lambda b, s, pt_ref: (0, 0)), per_b(prev[0]), per_b(prev[1])]
                + [page_spec(p) for p in range(npg)] + [page_spec(p) for p in range(npg)])
    rq = q.shape[1]
    grid_spec = pltpu.PrefetchScalarGridSpec(
        num_scalar_prefetch=1, grid=(nb, nsteps), in_specs=in_specs,
        out_specs=(pl.BlockSpec((None, rq, LANES), lambda b, s, pt_ref: (b, 0, 0)),
                   pl.BlockSpec((None, rq, w), lambda b, s, pt_ref: (b, 0, 0))))
    return pl.pallas_call(
        functools.partial(_dec_sb_kernel, npg=npg, ts=ts, first=first),
        grid_spec=grid_spec,
        out_shape=(jax.ShapeDtypeStruct((nb, rq, LANES), F32), jax.ShapeDtypeStruct((nb, rq, w), F32)),
        compiler_params=_cparams(("parallel", "arbitrary")), name="dec_sb_first" if first else "dec_sb_rest",
    )(ptf, q, u, prev[0], prev[1], *([ck] * npg), *([cv] * npg))


def _dec_sb_all(pt, q, own_k, own_v, ck, cv, layer, npg):
    n_pages = pt.shape[1]
    state = _dec_sb(pt, q, (own_k, own_v), ck, cv, layer, npg, n_pages, 1, True)
    if n_pages > npg:
        rest = lambda st: _dec_sb(pt, q, st, ck, cv, layer, npg, n_pages - npg, n_pages // npg - 1, False)
        state = lax.cond(jnp.max(state[0]) > SB_DONE, rest, lambda st: st, state)
    return state[1]


def _pad_rows(a, n):
    return jnp.pad(a, ((0, 0), (0, n - a.shape[1]), (0, 0)))


def _feature_major(a, n):
    at = jnp.swapaxes(a, 1, 2)
    return jnp.pad(at, ((0, 0), (0, 0), (0, n - at.shape[2])))


def _block_diag_queries(q, groups, width):
    nb, ts, _ = q.shape
    eye = jnp.eye(groups, dtype=q.dtype)
    qg = q.reshape(nb, ts, groups, width)
    out = jnp.einsum('btgw,hg->bhtgw', qg, eye)
    return out.reshape(nb, groups * ts, groups * width)


def kernel(x_prompt, x_sample, cache_diff_k, cache_diff_v, cache_mla, cache_sb_k, cache_sb_v, page_table,
           g_mix, g_ffn, w_in_even, diff_lambda, g_diff_sub, g_mla_q, g_mla_kv, w_mla_uq, w_mla_uk, w_mla_uv,
           w_out_even, w_in_odd, w_out_odd, w_ff1, w_ff2, g_final):
    b, t, d = x_prompt.shape
    nb, ts, _ = x_sample.shape
    n_pages = page_table.shape[1]
    past = n_pages * PAGE
    depth = g_mix.shape[0]
    n_even, n_odd = (depth + 1) // 2, depth // 2
    n_pool = cache_diff_k.shape[1]
    npg = _tile(n_pages, 8)
    npg_e = _tile(n_pages, 16)

    xp = x_prompt.reshape(b * t, d)
    xs = x_sample.reshape(nb * ts, d)
    tm_p = _tile(b * t, 512)
    tm_s = _tile(nb * ts, 512)
    tab_p = _rope_tables(jnp.arange(t))
    tab_s = jnp.tile(_rope_tables(past + jnp.arange(ts)), (tm_s // ts, 1))

    ck = jnp.transpose(cache_diff_k, (0, 1, 3, 4, 5, 2)).reshape(n_even, n_pool, N_A, PAGE)
    cv = cache_diff_v.reshape(n_even, n_pool, PAGE * H_A, 2 * DH_A)
    cm = jnp.swapaxes(cache_mla, 2, 3)
    csk = jnp.transpose(cache_sb_k, (0, 1, 3, 4, 2)).reshape(n_odd, n_pool, NK_C, PAGE)
    csv = jnp.transpose(cache_sb_v, (0, 1, 3, 4, 2)).reshape(n_odd, n_pool, NK_C, PAGE)

    row2 = lambda v: v.reshape(1, -1)
    outs_p = {k: [] for k in ("dk", "dv", "ml", "sk", "sv")}
    outs_s = {k: [] for k in ("dk", "dv", "ml", "sk", "sv")}

    for l in range(depth):
        w1 = w_ff1[l].astype(BF16)
        w2 = w_ff2[l].astype(BF16)
        final = l == depth - 1
        gf = row2(g_final)
        if l % 2 == 0:
            e = l // 2
            lam_init = 0.8 - 0.6 * math.exp(-0.3 * l)
            wi = w_in_even[e]
            c0 = 3 * N_A + Q_LORA + KV_LORA
            w_all = jnp.concatenate([wi[:, :c0]] + [wi[:, c0:]] * H_B, axis=1).astype(BF16)
            uq = w_mla_uq[e].reshape(Q_LORA, H_B, NOPE_B + ROPE_B)
            wuq = jnp.concatenate([uq[:, :, :NOPE_B].reshape(Q_LORA, H_B * NOPE_B),
                                   uq[:, :, NOPE_B:].reshape(Q_LORA, H_B * ROPE_B)], axis=1).astype(BF16)
            wuk = jnp.einsum('hcn,hg->hngc', w_mla_uk[e], jnp.eye(H_B, dtype=F32)).reshape(
                H_B * NOPE_B, H_B * KV_LORA).astype(BF16)
            wuv = w_mla_uv[e].astype(BF16)
            wo = w_out_even[e].astype(BF16)
            gq, gkv, gsub = row2(g_mla_q[e]), row2(g_mla_kv[e]), row2(g_diff_sub[e])
            lam_vec = diff_lambda[e]
            gm = row2(g_mix[l])

            kd, vd, ml, qd_b, kd_b, vd_b, qm_b, km_b = _even_proj(xp, gm, w_all, tab_p, gq, gkv, wuq, wuk, tm_p)
            outs_p["dk"].append(kd.reshape(b, t, H_A, 2, DH_A))
            outs_p["dv"].append(vd.reshape(b, t, H_A, 2 * DH_A))
            outs_p["ml"].append(ml.reshape(b, t, KV_LORA + ROPE_B))
            od = _diff_attn(qd_b, kd_b, vd_b, lam_vec, gsub, b, t, lam_init)
            om = _mla_attn(qm_b, km_b, wuv, b, t)
            xp = _post(xp, od, 0, om, 0, wo, row2(g_ffn[l]), w1, w2, gf, final)

            kd, vd, ml, qd_b, _, _, qm_b, _ = _even_proj(xs, gm, w_all, tab_s, gq, gkv, wuq, wuk, tm_s)
            outs_s["dk"].append(kd.reshape(nb, ts, H_A, 2, DH_A))
            outs_s["dv"].append(vd.reshape(nb, ts, H_A, 2 * DH_A))
            outs_s["ml"].append(ml.reshape(nb, ts, KV_LORA + ROPE_B))
            qbd = _block_diag_queries(qd_b.reshape(nb, ts, N_A), H_A * 2, DH_A)
            qm4 = qm_b.reshape(H_B, nb, ts, MLA_W)
            ql = jnp.transpose(qm4[..., :KV_LORA], (1, 0, 2, 3)).reshape(nb, H_B * ts, KV_LORA)
            qr = jnp.stack([qm4[h, :, :, KV_LORA + h * ROPE_B:KV_LORA + (h + 1) * ROPE_B] for h in range(H_B)],
                           axis=1).reshape(nb, H_B * ts, ROPE_B)
            od_s, ol_s = _dec_even(page_table, qbd, ql, qr,
                                   _feature_major(kd.reshape(nb, ts, N_A), PAGE),
                                   _pad_rows(vd.reshape(nb, ts * H_A, 2 * DH_A), PAGE * H_A),
                                   _feature_major(ml.reshape(nb, ts, KV_LORA + ROPE_B), PAGE),
                                   lam_vec, gsub, ck, cv, cm, e, npg_e, lam_init)
            ol_h = jnp.transpose(ol_s.reshape(nb, H_B, ts, KV_LORA), (1, 0, 2, 3)).reshape(H_B, nb * ts, KV_LORA)
            om_s = _uv(ol_h, wuv)
            xs = _post(xs, od_s.reshape(nb * ts, N_A), 0, om_s, 0, wo, row2(g_ffn[l]), w1, w2, gf, final)
        else:
            o = l // 2
            wi = w_in_odd[o]
            wq, wk, wv = wi[:, :NQ_C], wi[:, NQ_C:NQ_C + NK_C], wi[:, NQ_C + NK_C:]
            dup = lambda w: jnp.repeat(w.reshape(d, KVH_C, 1, DH_C), 2, axis=2).reshape(d, 2 * NK_C)
            w_all = jnp.concatenate([wq, wk, wv, dup(wk), dup(wv)], axis=1).astype(BF16)
            wo = w_out_odd[o].astype(BF16)
            gm = row2(g_mix[l])

            k, v, q_b, k2_b, v2_b = _odd_proj(xp, gm, w_all, tm_p)
            outs_p["sk"].append(k.reshape(b, t, KVH_C, DH_C))
            outs_p["sv"].append(v.reshape(b, t, KVH_C, DH_C))
            att = _sb_attn(q_b, k2_b, v2_b, b, t)
            xp = _post(xp, att, 0, att, 1, wo, row2(g_ffn[l]), w1, w2, gf, final)

            k, v, q_b, _, _ = _odd_proj(xs, gm, w_all, tm_s)
            outs_s["sk"].append(k.reshape(nb, ts, KVH_C, DH_C))
            outs_s["sv"].append(v.reshape(nb, ts, KVH_C, DH_C))
            qg = jnp.transpose(q_b.reshape(nb, ts, KVH_C, G_C, DH_C), (0, 2, 3, 1, 4))
            qbd = jnp.einsum('bkgtd,kj->bkgtjd', qg, jnp.eye(KVH_C, dtype=BF16)).reshape(
                nb, H_C * ts, NK_C)
            acc_s = _dec_sb_all(page_table, qbd, _feature_major(k.reshape(nb, ts, NK_C), PAGE),
                                _feature_major(v.reshape(nb, ts, NK_C), PAGE), csk, csv, o, npg)
            acc5 = acc_s.reshape(nb, KVH_C, G_C, ts, KVH_C, DH_C)
            att_s = jnp.stack([acc5[:, kh, :, :, kh, :] for kh in range(KVH_C)], axis=1)
            att_s = jnp.transpose(att_s, (0, 3, 1, 2, 4)).reshape(nb * ts, NQ_C)
            xs = _post(xs, att_s, 0, att_s, 1, wo, row2(g_ffn[l]), w1, w2, gf, final)

    st = lambda xs_: jnp.stack(xs_)
    return (xp.reshape(b, t, d), xs.reshape(nb, ts, d),
            st(outs_p["dk"]), st(outs_p["dv"]), st(outs_p["ml"]), st(outs_p["sk"]), st(outs_p["sv"]),
            st(outs_s["dk"]), st(outs_s["dv"]), st(outs_s["ml"]), st(outs_s["sk"]), st(outs_s["sv"]))
```

```python
import functools
import math

import jax
import jax.numpy as jnp
from jax import lax
from jax.experimental import pallas as pl
from jax.experimental.pallas import tpu as pltpu

F32 = jnp.float32
BF16 = jnp.bfloat16

EPS = 1e-6
NEG = -1e30
H_A = 4
DH_A = 64
ROT_A = DH_A // 4
ROPE_THETA = 500000.0
SCALE_A = DH_A ** -0.5
H_B = 4
Q_LORA = 384
KV_LORA = 256
NOPE_B = 64
ROPE_B = 32
V_B = 128
MLA_THETA = 10000.0
SCALE_B = (NOPE_B + ROPE_B) ** -0.5
H_C = 16
KVH_C = 4
G_C = H_C // KVH_C
DH_C = 64
SCALE_C = DH_C ** -0.5
PAGE = 128
LOG2E = math.log2(math.e)
QSCALE_A = SCALE_A * LOG2E
QSCALE_B = SCALE_B * LOG2E
QSCALE_C = SCALE_C * LOG2E

LANES = 128
VMEM_LIMIT = 48 * 1024 * 1024

N_A = H_A * 2 * DH_A
MLA_W = KV_LORA + LANES


def _tile(n, pref):
    t = min(n, pref)
    while n % t:
        t //= 2
    return t


def _dot(a, b):
    return jnp.dot(a, b, preferred_element_type=F32)


def _dot_nt(a, b):
    return lax.dot_general(a, b, (((1,), (1,)), ((), ())), preferred_element_type=F32)


def _rep(x, width):
    n = width // LANES
    return x if n == 1 else pltpu.repeat(x, n, 1)


def _store_row_groups(o_ref, y, width):
    tm = y.shape[0]
    g = y.shape[1] // width
    for j in range(g):
        o_ref[pl.ds(j, tm, stride=g), :] = y[:, j * width:(j + 1) * width]


def _rms(x, g):
    return x * lax.rsqrt(jnp.mean(x * x, axis=-1, keepdims=True) + EPS) * g


def _cparams(sem, vmem=VMEM_LIMIT):
    return pltpu.CompilerParams(dimension_semantics=sem, vmem_limit_bytes=vmem)


def _rope_tables(pos):
    posf = pos.astype(F32)[:, None]
    lane = jnp.arange(LANES)

    def tab(period, half, theta):
        inv = theta ** (-jnp.arange(half, dtype=F32) / half)
        ang = posf * inv[None, :]
        cos, sin = jnp.cos(ang), jnp.sin(ang)
        d = lane % period
        first = d < half
        second = (d >= half) & (d < 2 * half)
        idx = jnp.where(first, d, jnp.where(second, d - half, 0))
        c = jnp.where((first | second)[None, :], cos[:, idx], 1.0)
        s1 = jnp.where(first[None, :], -sin[:, idx], 0.0)
        s2 = jnp.where(second[None, :], sin[:, idx], 0.0)
        return [c, s1, s2]

    return jnp.concatenate(tab(DH_A, ROT_A // 2, ROPE_THETA) + tab(ROPE_B, ROPE_B // 2, MLA_THETA), axis=1)


def _rope(y, c, s1, s2, half):
    outs = []
    for k in range(y.shape[1] // LANES):
        yb = y[:, k * LANES:(k + 1) * LANES]
        outs.append(yb * c + pltpu.roll(yb, LANES - half, 1) * s1 + pltpu.roll(yb, half, 1) * s2)
    return outs[0] if len(outs) == 1 else jnp.concatenate(outs, axis=1)


def _even_proj_kernel(x_ref, g_ref, w_ref, tab_ref, gq_ref, gkv_ref, wuq_ref, wuk_ref,
                      kd_o, vd_o, mla_o, qd_b, kd_b, vd_b, qm_b, km_b):
    hb = _rms(x_ref[...], g_ref[...]).astype(BF16)
    tab = tab_ref[...]
    ca, s1a, s2a, cb, s1b, s2b = [tab[:, k * LANES:(k + 1) * LANES] for k in range(6)]
    o = 0
    qd = _rope(_dot(hb, w_ref[:, o:o + N_A]), ca, s1a, s2a, ROT_A // 2)
    qd_b[...] = (qd * QSCALE_A).astype(BF16)
    o += N_A
    kd = _rope(_dot(hb, w_ref[:, o:o + N_A]), ca, s1a, s2a, ROT_A // 2)
    _store_row_groups(kd_o, kd, DH_A)
    kd_b[...] = kd.astype(BF16)
    o += N_A
    vd = _dot(hb, w_ref[:, o:o + N_A])
    _store_row_groups(vd_o, vd, 2 * DH_A)
    vd_b[...] = vd.astype(BF16)
    o += N_A
    cq = _dot(hb, w_ref[:, o:o + Q_LORA])
    o += Q_LORA
    ckv = _dot(hb, w_ref[:, o:o + KV_LORA])
    o += KV_LORA
    kr4 = _dot(hb, w_ref[:, o:o + LANES])
    cqn = _rms(cq, gq_ref[...]).astype(BF16)
    qn = _dot(cqn, wuq_ref[:, 0:H_B * NOPE_B]).astype(BF16)
    qr = _dot(cqn, wuq_ref[:, H_B * NOPE_B:])
    qlat = _dot(qn, wuk_ref[...])
    qr = _rope(qr, cb, s1b, s2b, ROPE_B // 2)
    ckvn = _rms(ckv, gkv_ref[...])
    kr4 = _rope(kr4, cb, s1b, s2b, ROPE_B // 2)
    mla_o[:, 0:KV_LORA] = ckvn
    mla_o[:, KV_LORA:KV_LORA + ROPE_B] = kr4[:, 0:ROPE_B]
    km_b[:, 0:KV_LORA] = ckvn.astype(BF16)
    km_b[:, KV_LORA:] = kr4.astype(BF16)
    lane = lax.broadcasted_iota(jnp.int32, (1, LANES), 1)
    for h in range(H_B):
        qm_b[h, :, 0:KV_LORA] = (qlat[:, h * KV_LORA:(h + 1) * KV_LORA] * QSCALE_B).astype(BF16)
        qm_b[h, :, KV_LORA:] = jnp.where((lane >= h * ROPE_B) & (lane < (h + 1) * ROPE_B), qr * QSCALE_B, 0.0).astype(BF16)


def _even_proj(x, g, w, tab, gq, gkv, wuq, wuk, tm):
    r, d = x.shape
    nt = tab.shape[0] // tm
    wcols = w.shape[1]
    row = lambda n: pl.BlockSpec((tm, n), lambda i: (i, 0))
    full = lambda a: pl.BlockSpec(a.shape, lambda i: (0,) * a.ndim)
    out_shape = (
        jax.ShapeDtypeStruct((r * H_A * 2, DH_A), F32), jax.ShapeDtypeStruct((r * H_A, 2 * DH_A), F32),
        jax.ShapeDtypeStruct((r, KV_LORA + ROPE_B), F32),
        jax.ShapeDtypeStruct((r, N_A), BF16), jax.ShapeDtypeStruct((r, N_A), BF16),
        jax.ShapeDtypeStruct((r, N_A), BF16),
        jax.ShapeDtypeStruct((H_B, r, MLA_W), BF16), jax.ShapeDtypeStruct((r, MLA_W), BF16))
    out_specs = (pl.BlockSpec((tm * H_A * 2, DH_A), lambda i: (i, 0)),
                 pl.BlockSpec((tm * H_A, 2 * DH_A), lambda i: (i, 0)),
                 row(KV_LORA + ROPE_B), row(N_A), row(N_A), row(N_A),
                 pl.BlockSpec((H_B, tm, MLA_W), lambda i: (0, i, 0)), row(MLA_W))
    return pl.pallas_call(
        _even_proj_kernel, grid=(r // tm,),
        in_specs=[row(d), full(g), full(w), pl.BlockSpec((tm, tab.shape[1]), lambda i: (i % nt, 0)),
                  full(gq), full(gkv), full(wuq), full(wuk)],
        out_specs=out_specs, out_shape=out_shape,
        compiler_params=_cparams(("parallel",)), name="even_proj",
    )(x, g, w, tab, gq, gkv, wuq, wuk)


def _lam(lam_ref, lam_init):
    lf = lam_ref[...]
    a = jnp.sum(lf[0:1] * lf[1:2], axis=-1, keepdims=True)
    b = jnp.sum(lf[2:3] * lf[3:4], axis=-1, keepdims=True)
    return jnp.exp(a) - jnp.exp(b) + lam_init


def _softmax_step(s, vb, m_ref, l_ref, acc_ref):
    m_prev = m_ref[...]
    m_new = jnp.maximum(m_prev, jnp.max(s, axis=-1, keepdims=True))
    alpha = jnp.exp2(m_prev - m_new)
    p = jnp.exp2(s - _rep(m_new, s.shape[1]))
    l_ref[...] = alpha * l_ref[...] + jnp.sum(p, axis=-1, keepdims=True)
    pv = _dot(p.astype(BF16), vb)
    acc_ref[...] = _rep(alpha, acc_ref.shape[1]) * acc_ref[...] + pv
    m_ref[...] = m_new


def _attn_block(qq_ref, kb, vb, rows, cols, m_ref, l_ref, acc_ref, nsplit):
    n = qq_ref.shape[0] // nsplit
    for r in range(nsplit):
        sl = slice(r * n, (r + 1) * n)
        s = _dot_nt(qq_ref[sl, :], kb)
        if cols is not None:
            s = jnp.where(cols <= rows[sl], s, NEG)
        _softmax_step(s, vb, m_ref.at[sl, :], l_ref.at[sl, :], acc_ref.at[sl, :])


def _diff_attn_kernel(q_ref, k_ref, v_ref, lam_ref, gsub_ref, o_ref, qq_ref, m_ref, l_ref, acc_ref,
                      *, tq, tk, lam_init, nsplit):
    i = pl.program_id(2)
    lane = lax.broadcasted_iota(jnp.int32, (1, LANES), 1)
    q = q_ref[...]
    zero = jnp.zeros_like(q)
    qq_ref[0:tq, :] = jnp.where(lane < DH_A, q, zero)
    qq_ref[tq:2 * tq, :] = jnp.where(lane >= DH_A, q, zero)
    m_ref[...] = jnp.full(m_ref.shape, NEG, F32)
    l_ref[...] = jnp.zeros(l_ref.shape, F32)
    acc_ref[...] = jnp.zeros(acc_ref.shape, F32)

    rows = lax.broadcasted_iota(jnp.int32, (2 * tq, 1), 0) % tq

    def block(off, cols):
        _attn_block(qq_ref, k_ref[pl.ds(off, tk), :], v_ref[pl.ds(off, tk), :], rows, cols,
                    m_ref, l_ref, acc_ref, nsplit)

    def body(j, carry):
        block(pl.multiple_of(j * tk, tk), None)
        return carry

    lax.fori_loop(0, i * (tq // tk), body, 0)
    cols = lax.broadcasted_iota(jnp.int32, (1, tk), 1)
    for d in range(tq // tk):
        block(pl.multiple_of(i * tq + d * tk, tk), cols + d * tk)

    o = acc_ref[...] / l_ref[...]
    od = o[0:tq] - _lam(lam_ref, lam_init) * o[tq:2 * tq]
    o_ref[...] = (_rms(od, gsub_ref[...]) * (1.0 - lam_init)).astype(o_ref.dtype)


def _diff_attn(qd, kd, vd, lam_vec, gsub, b, t, lam_init):
    tq = _tile(t, 1024)
    tk = _tile(tq, 1024)
    nq = t // tq
    return pl.pallas_call(
        functools.partial(_diff_attn_kernel, tq=tq, tk=tk, lam_init=lam_init, nsplit=2),
        grid=(b, H_A, nq),
        in_specs=[pl.BlockSpec((tq, LANES), lambda bb, h, i: (bb * nq + i, h)),
                  pl.BlockSpec((t, LANES), lambda bb, h, i: (bb, h)),
                  pl.BlockSpec((t, LANES), lambda bb, h, i: (bb, h)),
                  pl.BlockSpec(lam_vec.shape, lambda bb, h, i: (0, 0)),
                  pl.BlockSpec(gsub.shape, lambda bb, h, i: (0, 0))],
        out_specs=pl.BlockSpec((tq, LANES), lambda bb, h, i: (bb * nq + i, h)),
        out_shape=jax.ShapeDtypeStruct((b * t, N_A), BF16),
        scratch_shapes=[pltpu.VMEM((2 * tq, LANES), BF16), pltpu.VMEM((2 * tq, LANES), F32),
                        pltpu.VMEM((2 * tq, LANES), F32), pltpu.VMEM((2 * tq, LANES), F32)],
        compiler_params=_cparams(("parallel", "parallel", "arbitrary")), name="diff_attn",
    )(qd, kd, vd, lam_vec, gsub)


def _mla_attn_kernel(q_ref, k_ref, wuv_ref, o_ref, qq_ref, m_ref, l_ref, acc_ref, *, tq, tk, nsplit):
    i = pl.program_id(1)
    for h in range(H_B):
        qq_ref[h * tq:(h + 1) * tq, :] = q_ref[h]
    m_ref[...] = jnp.full(m_ref.shape, NEG, F32)
    l_ref[...] = jnp.zeros(l_ref.shape, F32)
    acc_ref[...] = jnp.zeros(acc_ref.shape, F32)

    rows = lax.broadcasted_iota(jnp.int32, (H_B * tq, 1), 0) % tq

    def block(off, cols):
        kb = k_ref[pl.ds(off, tk), :]
        _attn_block(qq_ref, kb, kb[:, 0:KV_LORA], rows, cols, m_ref, l_ref, acc_ref, nsplit)

    def body(j, carry):
        block(pl.multiple_of(j * tk, tk), None)
        return carry

    lax.fori_loop(0, i * (tq // tk), body, 0)
    cols = lax.broadcasted_iota(jnp.int32, (1, tk), 1)
    for d in range(tq // tk):
        block(pl.multiple_of(i * tq + d * tk, tk), cols + d * tk)

    ol = (acc_ref[...] / _rep(l_ref[...], KV_LORA)).astype(BF16)
    for h in range(H_B):
        o_ref[:, h * V_B:(h + 1) * V_B] = _dot(ol[h * tq:(h + 1) * tq], wuv_ref[h]).astype(o_ref.dtype)


def _mla_attn(qm, km, wuv, b, t):
    tq = _tile(t, 512)
    tk = _tile(tq, 512)
    nq = t // tq
    return pl.pallas_call(
        functools.partial(_mla_attn_kernel, tq=tq, tk=tk, nsplit=2),
        grid=(b, nq),
        in_specs=[pl.BlockSpec((H_B, tq, MLA_W), lambda bb, i: (0, bb * nq + i, 0)),
                  pl.BlockSpec((t, MLA_W), lambda bb, i: (bb, 0)),
                  pl.BlockSpec(wuv.shape, lambda bb, i: (0, 0, 0))],
        out_specs=pl.BlockSpec((tq, H_B * V_B), lambda bb, i: (bb * nq + i, 0)),
        out_shape=jax.ShapeDtypeStruct((b * t, H_B * V_B), BF16),
        scratch_shapes=[pltpu.VMEM((H_B * tq, MLA_W), BF16), pltpu.VMEM((H_B * tq, LANES), F32),
                        pltpu.VMEM((H_B * tq, LANES), F32), pltpu.VMEM((H_B * tq, KV_LORA), F32)],
        compiler_params=_cparams(("parallel", "arbitrary")), name="mla_attn",
    )(qm, km, wuv)


def _post_kernel(x_ref, a1_ref, a2_ref, wo_ref, g_ref, w1_ref, w2_ref, gf_ref, o_ref,
                 x1_ref, hn_ref, acc_ref, *, final):
    j = pl.program_id(1)
    half = a1_ref.shape[1]

    @pl.when(j == 0)
    def _():
        mix = (_dot(a1_ref[...].astype(BF16), wo_ref[0:half, :])
               + _dot(a2_ref[...].astype(BF16), wo_ref[half:2 * half, :]))
        x1 = x_ref[...] + mix
        x1_ref[...] = x1
        hn_ref[...] = _rms(x1, g_ref[...]).astype(BF16)
        acc_ref[...] = jnp.zeros(acc_ref.shape, F32)

    h1 = jnp.maximum(_dot(hn_ref[...], w1_ref[...]), 0.0)
    acc_ref[...] += _dot((h1 * h1).astype(BF16), w2_ref[...])

    @pl.when(j == pl.num_programs(1) - 1)
    def _():
        y = x1_ref[...] + acc_ref[...]
        if final:
            y = _rms(y, gf_ref[...])
        o_ref[...] = y


def _post(x, a1, c1, a2, c2, wo, g, w1, w2, gf, final):
    r, d = x.shape
    tm = _tile(r, 512)
    dff = w1.shape[1]
    tf = _tile(dff, 1024)
    half = d // 2
    return pl.pallas_call(
        functools.partial(_post_kernel, final=final),
        grid=(r // tm, dff // tf),
        in_specs=[pl.BlockSpec((tm, d), lambda i, j: (i, 0)),
                  pl.BlockSpec((tm, half), lambda i, j: (i, c1)),
                  pl.BlockSpec((tm, half), lambda i, j: (i, c2)),
                  pl.BlockSpec(wo.shape, lambda i, j: (0, 0)),
                  pl.BlockSpec(g.shape, lambda i, j: (0, 0)),
                  pl.BlockSpec((d, tf), lambda i, j: (0, j)),
                  pl.BlockSpec((tf, d), lambda i, j: (j, 0)),
                  pl.BlockSpec(gf.shape, lambda i, j: (0, 0))],
        out_specs=pl.BlockSpec((tm, d), lambda i, j: (i, 0)),
        out_shape=jax.ShapeDtypeStruct((r, d), F32),
        scratch_shapes=[pltpu.VMEM((tm, d), F32), pltpu.VMEM((tm, d), BF16), pltpu.VMEM((tm, d), F32)],
        compiler_params=_cparams(("parallel", "arbitrary")), name="post_final" if final else "post",
    )(x, a1, a2, wo, g, w1, w2, gf)


NQ_C = H_C * DH_C
NK_C = KVH_C * DH_C


def _odd_proj_kernel(x_ref, g_ref, w_ref, k_o, v_o, q_b, k2_b, v2_b):
    hb = _rms(x_ref[...], g_ref[...]).astype(BF16)
    o = 0
    q_b[...] = (_dot(hb, w_ref[:, o:o + NQ_C]) * QSCALE_C).astype(BF16)
    o += NQ_C
    _store_row_groups(k_o, _dot(hb, w_ref[:, o:o + NK_C]), DH_C)
    o += NK_C
    _store_row_groups(v_o, _dot(hb, w_ref[:, o:o + NK_C]), DH_C)
    o += NK_C
    k2_b[...] = _dot(hb, w_ref[:, o:o + 2 * NK_C]).astype(BF16)
    o += 2 * NK_C
    v2_b[...] = _dot(hb, w_ref[:, o:o + 2 * NK_C]).astype(BF16)


def _odd_proj(x, g, w, tm):
    r, d = x.shape
    row = lambda n: pl.BlockSpec((tm, n), lambda i: (i, 0))
    full = lambda a: pl.BlockSpec(a.shape, lambda i: (0,) * a.ndim)
    return pl.pallas_call(
        _odd_proj_kernel, grid=(r // tm,),
        in_specs=[row(d), full(g), full(w)],
        out_specs=(pl.BlockSpec((tm * KVH_C, DH_C), lambda i: (i, 0)),
                   pl.BlockSpec((tm * KVH_C, DH_C), lambda i: (i, 0)),
                   row(NQ_C), row(2 * NK_C), row(2 * NK_C)),
        out_shape=(jax.ShapeDtypeStruct((r * KVH_C, DH_C), F32), jax.ShapeDtypeStruct((r * KVH_C, DH_C), F32),
                   jax.ShapeDtypeStruct((r, NQ_C), BF16), jax.ShapeDtypeStruct((r, 2 * NK_C), BF16),
                   jax.ShapeDtypeStruct((r, 2 * NK_C), BF16)),
        compiler_params=_cparams(("parallel",)), name="odd_proj",
    )(x, g, w)


SB_DONE = -150.0


def _sb_live(c_ref):
    return jnp.max(c_ref[...]) > SB_DONE


def _sb_step(z, vb, u, mask, c_ref, acc_ref, v_feature_major=False):
    tk = z.shape[1]
    ls = jnp.minimum(z, 0.0) - jnp.log2(1.0 + jnp.exp2(-jnp.abs(z)))
    lneg = ls - z
    if mask is not None:
        lneg = jnp.where(mask, lneg, 0.0)
    hi = lneg.astype(BF16)
    lo = (lneg - hi.astype(F32)).astype(BF16)
    suf = _dot(hi, u) + _dot(lo, u)
    c = c_ref[...]
    a = jnp.exp2(ls + suf + _rep(c, tk))
    if mask is not None:
        a = jnp.where(mask, a, 0.0)
    ab = a.astype(BF16)
    acc_ref[...] += _dot_nt(ab, vb) if v_feature_major else _dot(ab, vb)
    c_ref[...] = c + (suf[:, 0:1] + lneg[:, 0:1])


def _sb_attn_kernel(q_ref, k_ref, v_ref, u_ref, o_ref, qq_ref, c_ref, acc_ref, *, tq):
    i = pl.program_id(2)
    lane = lax.broadcasted_iota(jnp.int32, (1, LANES), 1)
    for g in range(G_C):
        qb = q_ref[:, (g // 2) * LANES:(g // 2 + 1) * LANES]
        keep = (lane < DH_C) if g % 2 == 0 else (lane >= DH_C)
        qq_ref[g * tq:(g + 1) * tq, :] = jnp.where(keep, qb, jnp.zeros_like(qb))
    c_ref[...] = jnp.zeros(c_ref.shape, F32)
    acc_ref[...] = jnp.zeros(acc_ref.shape, F32)

    rows = lax.broadcasted_iota(jnp.int32, (G_C * tq, 1), 0) % tq

    def block(off, n, mask):
        z = _dot_nt(qq_ref[...], k_ref[pl.ds(off, n), :])
        _sb_step(z, v_ref[pl.ds(off, n), :], u_ref[0:n, 0:n], mask, c_ref, acc_ref)

    @pl.when(i == 0)
    def _():
        block(0, tq, lax.broadcasted_iota(jnp.int32, (1, tq), 1) < rows)

    @pl.when(i > 0)
    def _():
        block(pl.multiple_of((i - 1) * tq, tq), 2 * tq,
              lax.broadcasted_iota(jnp.int32, (1, 2 * tq), 1) < rows + tq)

    def cond(carry):
        j, live = carry
        return jnp.logical_and(j >= 0, live)

    def body(carry):
        j, _ = carry
        block(pl.multiple_of(j * tq, tq), tq, None)
        return j - 1, _sb_live(c_ref)

    lax.while_loop(cond, body, (i - 2, _sb_live(c_ref)))
    acc = acc_ref[...]
    for c2 in range(G_C // 2):
        o_ref[:, c2 * LANES:(c2 + 1) * LANES] = jnp.where(
            lane < DH_C, acc[(2 * c2) * tq:(2 * c2 + 1) * tq], acc[(2 * c2 + 1) * tq:(2 * c2 + 2) * tq]
        ).astype(o_ref.dtype)


def _tri(tk):
    j = jnp.arange(tk)
    return (j[:, None] > j[None, :]).astype(BF16)


def _sb_attn(q, k2, v2, b, t):
    tq = _tile(t, 256)
    nq = t // tq
    u = _tri(2 * tq)
    return pl.pallas_call(
        functools.partial(_sb_attn_kernel, tq=tq),
        grid=(b, KVH_C, nq),
        in_specs=[pl.BlockSpec((tq, G_C * DH_C), lambda bb, h, i: (bb * nq + i, h)),
                  pl.BlockSpec((t, LANES), lambda bb, h, i: (bb, h)),
                  pl.BlockSpec((t, LANES), lambda bb, h, i: (bb, h)),
                  pl.BlockSpec(u.shape, lambda bb, h, i: (0, 0))],
        out_specs=pl.BlockSpec((tq, G_C * DH_C), lambda bb, h, i: (bb * nq + i, h)),
        out_shape=jax.ShapeDtypeStruct((b * t, NQ_C), BF16),
        scratch_shapes=[pltpu.VMEM((G_C * tq, LANES), BF16), pltpu.VMEM((G_C * tq, LANES), F32),
                        pltpu.VMEM((G_C * tq, LANES), F32)],
        compiler_params=_cparams(("parallel", "parallel", "arbitrary")), name="sb_attn",
    )(q, k2, v2, u)


def _dec_even_kernel(pt_ref, qd_ref, ql_ref, qr_ref, ok_ref, ov_ref, om_ref, lam_ref, gsub_ref, *rest,
                     npg, ts, lam_init):
    kp = rest[0:npg]
    vp = rest[npg:2 * npg]
    mp = rest[2 * npg:3 * npg]
    od_ref, ol_ref = rest[3 * npg:3 * npg + 2]
    md_ref, ld_ref, accd_ref, mm_ref, lm_ref, accm_ref = rest[3 * npg + 2:]
    s_id = pl.program_id(1)
    rows_h = 2 * ts

    @pl.when(s_id == 0)
    def _():
        md_ref[...] = jnp.full(md_ref.shape, NEG, F32)
        ld_ref[...] = jnp.zeros(ld_ref.shape, F32)
        accd_ref[...] = jnp.zeros(accd_ref.shape, F32)
        mm_ref[...] = jnp.full(mm_ref.shape, NEG, F32)
        lm_ref[...] = jnp.zeros(lm_ref.shape, F32)
        accm_ref[...] = jnp.zeros(accm_ref.shape, F32)

    def softmax(s, m_ref, l_ref):
        m_prev = m_ref[...]
        m_new = jnp.maximum(m_prev, jnp.max(s, axis=-1, keepdims=True))
        alpha = jnp.exp2(m_prev - m_new)
        p = jnp.exp2(s - _rep(m_new, s.shape[1]))
        l_ref[...] = alpha * l_ref[...] + jnp.sum(p, axis=-1, keepdims=True)
        m_ref[...] = m_new
        return p.astype(BF16), alpha

    def cat(xs, axis):
        return xs[0] if len(xs) == 1 else jnp.concatenate(xs, axis=axis)

    def update(ks, vs, ms, dmask, mmask):
        sd = _dot(qd_ref[...], cat([k[...].astype(BF16) for k in ks], 1))
        if dmask is not None:
            sd = jnp.where(dmask, sd, NEG)
        pd, alpha = softmax(sd, md_ref, ld_ref)
        for h in range(H_A):
            vh = cat([v[pl.ds(h, PAGE, stride=H_A), :].astype(BF16) for v in vs], 0)
            r = slice(h * rows_h, (h + 1) * rows_h)
            accd_ref[r, :] = alpha[r] * accd_ref[r, :] + _dot(pd[r], vh)
        lat = cat([m[0:KV_LORA, :].astype(BF16) for m in ms], 1)
        rope = cat([m[KV_LORA:KV_LORA + ROPE_B, :].astype(BF16) for m in ms], 1)
        sm = _dot(ql_ref[...], lat) + _dot(qr_ref[...], rope)
        if mmask is not None:
            sm = jnp.where(mmask, sm, NEG)
        pm, alpha = softmax(sm, mm_ref, lm_ref)
        accm_ref[...] = _rep(alpha, KV_LORA) * accm_ref[...] + _dot_nt(pm, lat)

    update(kp, vp, mp, None, None)

    @pl.when(s_id == pl.num_programs(1) - 1)
    def _():
        key = lax.broadcasted_iota(jnp.int32, (1, PAGE), 1)
        tok_d = lax.broadcasted_iota(jnp.int32, (H_A * rows_h, 1), 0) % ts
        tok_m = lax.broadcasted_iota(jnp.int32, (H_B * ts, 1), 0) % ts
        update([ok_ref], [ov_ref], [om_ref], key <= tok_d, key <= tok_m)
        o = accd_ref[...] / ld_ref[...]
        lam = _lam(lam_ref, lam_init)
        for h in range(H_A):
            od = o[h * rows_h:h * rows_h + ts] - lam * o[h * rows_h + ts:(h + 1) * rows_h]
            od_ref[:, h * LANES:(h + 1) * LANES] = _rms(od, gsub_ref[...]) * (1.0 - lam_init)
        ol_ref[...] = accm_ref[...] / _rep(lm_ref[...], KV_LORA)


def _dec_even(pt, qd, ql, qr, own_k, own_v, own_m, lam_vec, gsub, ck, cv, cm, layer, npg, lam_init):
    nb, n_pages = pt.shape
    ts = qd.shape[1] // (H_A * 2)
    nsteps = n_pages // npg
    ptf = pt.reshape(-1)

    def page_spec(a, p):
        return pl.BlockSpec((None, None) + a.shape[2:],
                            lambda b, s, pt_ref: (layer, pt_ref[b * n_pages + s * npg + p], 0, 0))

    per_b = lambda a: pl.BlockSpec((None,) + a.shape[1:], lambda b, s, pt_ref: (b,) + (0,) * (a.ndim - 1))
    full = lambda a: pl.BlockSpec(a.shape, lambda b, s, pt_ref: (0,) * a.ndim)
    in_specs = ([per_b(qd), per_b(ql), per_b(qr), per_b(own_k), per_b(own_v), per_b(own_m),
                 full(lam_vec), full(gsub)]
                + [page_spec(ck, p) for p in range(npg)]
                + [page_spec(cv, p) for p in range(npg)]
                + [page_spec(cm, p) for p in range(npg)])
    rd, rm = qd.shape[1], ql.shape[1]
    grid_spec = pltpu.PrefetchScalarGridSpec(
        num_scalar_prefetch=1, grid=(nb, nsteps), in_specs=in_specs,
        out_specs=(pl.BlockSpec((None, ts, N_A), lambda b, s, pt_ref: (b, 0, 0)),
                   pl.BlockSpec((None, rm, KV_LORA), lambda b, s, pt_ref: (b, 0, 0))),
        scratch_shapes=[pltpu.VMEM((rd, LANES), F32), pltpu.VMEM((rd, LANES), F32), pltpu.VMEM((rd, LANES), F32),
                        pltpu.VMEM((rm, LANES), F32), pltpu.VMEM((rm, LANES), F32), pltpu.VMEM((rm, KV_LORA), F32)])
    return pl.pallas_call(
        functools.partial(_dec_even_kernel, npg=npg, ts=ts, lam_init=lam_init),
        grid_spec=grid_spec,
        out_shape=(jax.ShapeDtypeStruct((nb, ts, N_A), F32), jax.ShapeDtypeStruct((nb, rm, KV_LORA), F32)),
        compiler_params=_cparams(("parallel", "arbitrary")), name="dec_even",
    )(ptf, qd, ql, qr, own_k, own_v, own_m, lam_vec, gsub, *([ck] * npg), *([cv] * npg), *([cm] * npg))


def _uv_kernel(ol_ref, wuv_ref, o_ref):
    o_ref[...] = _dot(ol_ref[...].astype(BF16), wuv_ref[...]).astype(o_ref.dtype)


def _uv(ol, wuv):
    _, r, _ = ol.shape
    return pl.pallas_call(
        _uv_kernel, grid=(H_B,),
        in_specs=[pl.BlockSpec((None, r, KV_LORA), lambda h: (h, 0, 0)),
                  pl.BlockSpec((None, KV_LORA, V_B), lambda h: (h, 0, 0))],
        out_specs=pl.BlockSpec((r, V_B), lambda h: (0, h)),
        out_shape=jax.ShapeDtypeStruct((r, H_B * V_B), BF16),
        compiler_params=_cparams(("parallel",)), name="mla_uv",
    )(ol, wuv)


def _dec_sb_kernel(pt_ref, q_ref, u_ref, *rest, npg, ts, first):
    prev = rest[0:2]
    kp = rest[2:2 + npg]
    vp = rest[2 + npg:2 + 2 * npg]
    c_ref, acc_ref = rest[2 + 2 * npg:]
    s_id = pl.program_id(1)

    def block(pages, mask):
        cat = lambda xs: xs[0] if len(xs) == 1 else jnp.concatenate(xs, axis=1)
        n = len(pages) * PAGE
        z = _dot(q_ref[...], cat([k[...].astype(BF16) for k, _ in pages]))
        _sb_step(z, cat([v[...].astype(BF16) for _, v in pages]), u_ref[0:n, 0:n], mask, c_ref, acc_ref,
                 v_feature_major=True)

    todo = list(zip(kp, vp))
    if first:
        c_ref[...] = jnp.zeros(c_ref.shape, F32)
        acc_ref[...] = jnp.zeros(acc_ref.shape, F32)
        key = lax.broadcasted_iota(jnp.int32, (1, 2 * PAGE), 1)
        tok = lax.broadcasted_iota(jnp.int32, (H_C * ts, 1), 0) % ts
        block([todo[0], (prev[0], prev[1])], key < tok + PAGE)
        todo = todo[1:]
    else:
        @pl.when(s_id == 0)
        def _():
            c_ref[...] = prev[0][...]
            acc_ref[...] = prev[1][...]

    @pl.when(_sb_live(c_ref))
    def _():
        for p in range(0, len(todo), 2):
            @pl.when(_sb_live(c_ref))
            def _():
                block(todo[p:p + 2][::-1], None)


def _dec_sb(pt, q, prev, ck, cv, layer, npg, hi, nsteps, first):
    nb, n_pages = pt.shape
    ts = q.shape[1] // H_C
    ptf = pt.reshape(-1)
    w = ck.shape[-2]
    assert nsteps == 1 or not first
    u = _tri(2 * PAGE)

    def page_spec(p):
        return pl.BlockSpec((None, None, w, PAGE),
                            lambda b, s, pt_ref: (layer, pt_ref[b * n_pages + hi - 1 - (s * npg + p)], 0, 0))

    per_b = lambda a: pl.BlockSpec((None,) + a.shape[1:], lambda b, s, pt_ref: (b,) + (0,) * (a.ndim - 1))
    in_specs = ([per_b(q), pl.BlockSpec(u.shape, lambda b, s, pt_ref: (0, 0)), per_b(prev[0]), per_b(prev[1])]
                + [page_spec(p) for p in range(npg)] + [page_spec(p) for p in range(npg)])
    rq = q.shape[1]
    grid_spec = pltpu.PrefetchScalarGridSpec(
        num_scalar_prefetch=1, grid=(nb, nsteps), in_specs=in_specs,
        out_specs=(pl.BlockSpec((None, rq, LANES), lambda b, s, pt_ref: (b, 0, 0)),
                   pl.BlockSpec((None, rq, w), lambda b, s, pt_ref: (b, 0, 0))))
    return pl.pallas_call(
        functools.partial(_dec_sb_kernel, npg=npg, ts=ts, first=first),
        grid_spec=grid_spec,
        out_shape=(jax.ShapeDtypeStruct((nb, rq, LANES), F32), jax.ShapeDtypeStruct((nb, rq, w), F32)),
        compiler_params=_cparams(("parallel", "arbitrary")), name="dec_sb_first" if first else "dec_sb_rest",
    )(ptf, q, u, prev[0], prev[1], *([ck] * npg), *([cv] * npg))


def _dec_sb_all(pt, q, own_k, own_v, ck, cv, layer, npg):
    n_pages = pt.shape[1]
    state = _dec_sb(pt, q, (own_k, own_v), ck, cv, layer, npg, n_pages, 1, True)
    if n_pages > npg:
        rest = lambda st: _dec_sb(pt, q, st, ck, cv, layer, npg, n_pages - npg, n_pages // npg - 1, False)
        state = lax.cond(jnp.max(state[0]) > SB_DONE, rest, lambda st: st, state)
    return state[1]


def _pad_rows(a, n):
    return jnp.pad(a, ((0, 0), (0, n - a.shape[1]), (0, 0)))


def _feature_major(a, n):
    at = jnp.swapaxes(a.astype(BF16), 1, 2)
    return jnp.pad(at, ((0, 0), (0, 0), (0, n - at.shape[2])))


def _block_diag_queries(q, groups, width):
    nb, ts, _ = q.shape
    eye = jnp.eye(groups, dtype=q.dtype)
    qg = q.reshape(nb, ts, groups, width)
    out = jnp.einsum('btgw,hg->bhtgw', qg, eye)
    return out.reshape(nb, groups * ts, groups * width)


def kernel(x_prompt, x_sample, cache_diff_k, cache_diff_v, cache_mla, cache_sb_k, cache_sb_v, page_table,
           g_mix, g_ffn, w_in_even, diff_lambda, g_diff_sub, g_mla_q, g_mla_kv, w_mla_uq, w_mla_uk, w_mla_uv,
           w_out_even, w_in_odd, w_out_odd, w_ff1, w_ff2, g_final):
    b, t, d = x_prompt.shape
    nb, ts, _ = x_sample.shape
    n_pages = page_table.shape[1]
    past = n_pages * PAGE
    depth = g_mix.shape[0]
    n_even, n_odd = (depth + 1) // 2, depth // 2
    n_pool = cache_diff_k.shape[1]
    npg = _tile(n_pages, 8)
    npg_e = _tile(n_pages, 16)

    xp = x_prompt.reshape(b * t, d)
    xs = x_sample.reshape(nb * ts, d)
    tm_p = _tile(b * t, 512)
    tm_s = _tile(nb * ts, 512)
    tab_p = _rope_tables(jnp.arange(t))
    tab_s = jnp.tile(_rope_tables(past + jnp.arange(ts)), (tm_s // ts, 1))

    ck = jnp.transpose(cache_diff_k, (0, 1, 3, 4, 5, 2)).reshape(n_even, n_pool, N_A, PAGE)
    cv = cache_diff_v.reshape(n_even, n_pool, PAGE * H_A, 2 * DH_A)
    cm = jnp.swapaxes(cache_mla, 2, 3)
    csk = jnp.transpose(cache_sb_k, (0, 1, 3, 4, 2)).reshape(n_odd, n_pool, NK_C, PAGE)
    csv = jnp.transpose(cache_sb_v, (0, 1, 3, 4, 2)).reshape(n_odd, n_pool, NK_C, PAGE)

    row2 = lambda v: v.reshape(1, -1)
    outs_p = {k: [] for k in ("dk", "dv", "ml", "sk", "sv")}
    outs_s = {k: [] for k in ("dk", "dv", "ml", "sk", "sv")}

    for l in range(depth):
        w1 = w_ff1[l].astype(BF16)
        w2 = w_ff2[l].astype(BF16)
        final = l == depth - 1
        gf = row2(g_final)
        if l % 2 == 0:
            e = l // 2
            lam_init = 0.8 - 0.6 * math.exp(-0.3 * l)
            wi = w_in_even[e]
            c0 = 3 * N_A + Q_LORA + KV_LORA
            w_all = jnp.concatenate([wi[:, :c0]] + [wi[:, c0:]] * H_B, axis=1).astype(BF16)
            uq = w_mla_uq[e].reshape(Q_LORA, H_B, NOPE_B + ROPE_B)
            wuq = jnp.concatenate([uq[:, :, :NOPE_B].reshape(Q_LORA, H_B * NOPE_B),
                                   uq[:, :, NOPE_B:].reshape(Q_LORA, H_B * ROPE_B)], axis=1).astype(BF16)
            wuk = jnp.einsum('hcn,hg->hngc', w_mla_uk[e], jnp.eye(H_B, dtype=F32)).reshape(
                H_B * NOPE_B, H_B * KV_LORA).astype(BF16)
            wuv = w_mla_uv[e].astype(BF16)
            wo = w_out_even[e].astype(BF16)
            gq, gkv, gsub = row2(g_mla_q[e]), row2(g_mla_kv[e]), row2(g_diff_sub[e])
            lam_vec = diff_lambda[e]
            gm = row2(g_mix[l])

            kd, vd, ml, qd_b, kd_b, vd_b, qm_b, km_b = _even_proj(xp, gm, w_all, tab_p, gq, gkv, wuq, wuk, tm_p)
            outs_p["dk"].append(kd.reshape(b, t, H_A, 2, DH_A))
            outs_p["dv"].append(vd.reshape(b, t, H_A, 2 * DH_A))
            outs_p["ml"].append(ml.reshape(b, t, KV_LORA + ROPE_B))
            od = _diff_attn(qd_b, kd_b, vd_b, lam_vec, gsub, b, t, lam_init)
            om = _mla_attn(qm_b, km_b, wuv, b, t)
            xp = _post(xp, od, 0, om, 0, wo, row2(g_ffn[l]), w1, w2, gf, final)

            kd, vd, ml, qd_b, _, _, qm_b, _ = _even_proj(xs, gm, w_all, tab_s, gq, gkv, wuq, wuk, tm_s)
            outs_s["dk"].append(kd.reshape(nb, ts, H_A, 2, DH_A))
            outs_s["dv"].append(vd.reshape(nb, ts, H_A, 2 * DH_A))
            outs_s["ml"].append(ml.reshape(nb, ts, KV_LORA + ROPE_B))
            qbd = _block_diag_queries(qd_b.reshape(nb, ts, N_A), H_A * 2, DH_A)
            qm4 = qm_b.reshape(H_B, nb, ts, MLA_W)
            ql = jnp.transpose(qm4[..., :KV_LORA], (1, 0, 2, 3)).reshape(nb, H_B * ts, KV_LORA)
            qr = jnp.stack([qm4[h, :, :, KV_LORA + h * ROPE_B:KV_LORA + (h + 1) * ROPE_B] for h in range(H_B)],
                           axis=1).reshape(nb, H_B * ts, ROPE_B)
            od_s, ol_s = _dec_even(page_table, qbd, ql, qr,
                                   _feature_major(kd.reshape(nb, ts, N_A), PAGE),
                                   _pad_rows(vd.reshape(nb, ts * H_A, 2 * DH_A), PAGE * H_A),
                                   _feature_major(ml.reshape(nb, ts, KV_LORA + ROPE_B), PAGE),
                                   lam_vec, gsub, ck, cv, cm, e, npg_e, lam_init)
            ol_h = jnp.transpose(ol_s.reshape(nb, H_B, ts, KV_LORA), (1, 0, 2, 3)).reshape(H_B, nb * ts, KV_LORA)
            om_s = _uv(ol_h, wuv)
            xs = _post(xs, od_s.reshape(nb * ts, N_A), 0, om_s, 0, wo, row2(g_ffn[l]), w1, w2, gf, final)
        else:
            o = l // 2
            wi = w_in_odd[o]
            wq, wk, wv = wi[:, :NQ_C], wi[:, NQ_C:NQ_C + NK_C], wi[:, NQ_C + NK_C:]
            dup = lambda w: jnp.repeat(w.reshape(d, KVH_C, 1, DH_C), 2, axis=2).reshape(d, 2 * NK_C)
            w_all = jnp.concatenate([wq, wk, wv, dup(wk), dup(wv)], axis=1).astype(BF16)
            wo = w_out_odd[o].astype(BF16)
            gm = row2(g_mix[l])

            k, v, q_b, k2_b, v2_b = _odd_proj(xp, gm, w_all, tm_p)
            outs_p["sk"].append(k.reshape(b, t, KVH_C, DH_C))
            outs_p["sv"].append(v.reshape(b, t, KVH_C, DH_C))
            att = _sb_attn(q_b, k2_b, v2_b, b, t)
            xp = _post(xp, att, 0, att, 1, wo, row2(g_ffn[l]), w1, w2, gf, final)

            k, v, q_b, _, _ = _odd_proj(xs, gm, w_all, tm_s)
            outs_s["sk"].append(k.reshape(nb, ts, KVH_C, DH_C))
            outs_s["sv"].append(v.reshape(nb, ts, KVH_C, DH_C))
            qg = jnp.transpose(q_b.reshape(nb, ts, KVH_C, G_C, DH_C), (0, 2, 3, 1, 4))
            qbd = jnp.einsum('bkgtd,kj->bkgtjd', qg, jnp.eye(KVH_C, dtype=BF16)).reshape(
                nb, H_C * ts, NK_C)
            acc_s = _dec_sb_all(page_table, qbd, _feature_major(k.reshape(nb, ts, NK_C), PAGE),
                                _feature_major(v.reshape(nb, ts, NK_C), PAGE), csk, csv, o, npg)
            acc5 = acc_s.reshape(nb, KVH_C, G_C, ts, KVH_C, DH_C)
            att_s = jnp.stack([acc5[:, kh, :, :, kh, :] for kh in range(KVH_C)], axis=1)
            att_s = jnp.transpose(att_s, (0, 3, 1, 2, 4)).reshape(nb * ts, NQ_C)
            xs = _post(xs, att_s, 0, att_s, 1, wo, row2(g_ffn[l]), w1, w2, gf, final)

    st = lambda xs_: jnp.stack(xs_)
    return (xp.reshape(b, t, d), xs.reshape(nb, ts, d),
            st(outs_p["dk"]), st(outs_p["dv"]), st(outs_p["ml"]), st(outs_p["sk"]), st(outs_p["sv"]),
            st(outs_s["dk"]), st(outs_s["dv"]), st(outs_s["ml"]), st(outs_s["sk"]), st(outs_s["sv"]))
```

```python
import functools
import math

import jax
import jax.numpy as jnp
from jax import lax
from jax.experimental import pallas as pl
from jax.experimental.pallas import tpu as pltpu

F32 = jnp.float32
BF16 = jnp.bfloat16

EPS = 1e-6
NEG = -1e30
H_A = 4
DH_A = 64
ROT_A = DH_A // 4
ROPE_THETA = 500000.0
SCALE_A = DH_A ** -0.5
H_B = 4
Q_LORA = 384
KV_LORA = 256
NOPE_B = 64
ROPE_B = 32
V_B = 128
MLA_THETA = 10000.0
SCALE_B = (NOPE_B + ROPE_B) ** -0.5
H_C = 16
KVH_C = 4
G_C = H_C // KVH_C
DH_C = 64
SCALE_C = DH_C ** -0.5
PAGE = 128
LOG2E = math.log2(math.e)
QSCALE_A = SCALE_A * LOG2E
QSCALE_B = SCALE_B * LOG2E
QSCALE_C = SCALE_C * LOG2E

LANES = 128
VMEM_LIMIT = 48 * 1024 * 1024

N_A = H_A * 2 * DH_A
MLA_W = KV_LORA + LANES


def _tile(n, pref):
    t = min(n, pref)
    while n % t:
        t //= 2
    return t


def _dot(a, b):
    return jnp.dot(a, b, preferred_element_type=F32)


def _dot_nt(a, b):
    return lax.dot_general(a, b, (((1,), (1,)), ((), ())), preferred_element_type=F32)


def _rep(x, width):
    n = width // LANES
    return x if n == 1 else pltpu.repeat(x, n, 1)


def _store_row_groups(o_ref, y, width):
    tm = y.shape[0]
    g = y.shape[1] // width
    for j in range(g):
        o_ref[pl.ds(j, tm, stride=g), :] = y[:, j * width:(j + 1) * width]


def _rms(x, g):
    return x * lax.rsqrt(jnp.mean(x * x, axis=-1, keepdims=True) + EPS) * g


def _cparams(sem, vmem=VMEM_LIMIT):
    return pltpu.CompilerParams(dimension_semantics=sem, vmem_limit_bytes=vmem)


def _rope_tables(pos):
    posf = pos.astype(F32)[:, None]
    lane = jnp.arange(LANES)

    def tab(period, half, theta):
        inv = theta ** (-jnp.arange(half, dtype=F32) / half)
        ang = posf * inv[None, :]
        cos, sin = jnp.cos(ang), jnp.sin(ang)
        d = lane % period
        first = d < half
        second = (d >= half) & (d < 2 * half)
        idx = jnp.where(first, d, jnp.where(second, d - half, 0))
        c = jnp.where((first | second)[None, :], cos[:, idx], 1.0)
        s1 = jnp.where(first[None, :], -sin[:, idx], 0.0)
        s2 = jnp.where(second[None, :], sin[:, idx], 0.0)
        return [c, s1, s2]

    return jnp.concatenate(tab(DH_A, ROT_A // 2, ROPE_THETA) + tab(ROPE_B, ROPE_B // 2, MLA_THETA), axis=1)


def _rope(y, c, s1, s2, half):
    outs = []
    for k in range(y.shape[1] // LANES):
        yb = y[:, k * LANES:(k + 1) * LANES]
        outs.append(yb * c + pltpu.roll(yb, LANES - half, 1) * s1 + pltpu.roll(yb, half, 1) * s2)
    return outs[0] if len(outs) == 1 else jnp.concatenate(outs, axis=1)


def _even_proj_kernel(x_ref, g_ref, w_ref, tab_ref, gq_ref, gkv_ref, wuq_ref, wuk_ref,
                      kd_o, vd_o, mla_o, qd_b, kd_b, vd_b, qm_b, km_b):
    hb = _rms(x_ref[...], g_ref[...]).astype(BF16)
    tab = tab_ref[...]
    ca, s1a, s2a, cb, s1b, s2b = [tab[:, k * LANES:(k + 1) * LANES] for k in range(6)]
    o = 0
    qd = _rope(_dot(hb, w_ref[:, o:o + N_A]), ca, s1a, s2a, ROT_A // 2)
    qd_b[...] = (qd * QSCALE_A).astype(BF16)
    o += N_A
    kd = _rope(_dot(hb, w_ref[:, o:o + N_A]), ca, s1a, s2a, ROT_A // 2)
    _store_row_groups(kd_o, kd, DH_A)
    kd_b[...] = kd.astype(BF16)
    o += N_A
    vd = _dot(hb, w_ref[:, o:o + N_A])
    _store_row_groups(vd_o, vd, 2 * DH_A)
    vd_b[...] = vd.astype(BF16)
    o += N_A
    cq = _dot(hb, w_ref[:, o:o + Q_LORA])
    o += Q_LORA
    ckv = _dot(hb, w_ref[:, o:o + KV_LORA])
    o += KV_LORA
    kr4 = _dot(hb, w_ref[:, o:o + LANES])
    cqn = _rms(cq, gq_ref[...]).astype(BF16)
    qn = _dot(cqn, wuq_ref[:, 0:H_B * NOPE_B]).astype(BF16)
    qr = _dot(cqn, wuq_ref[:, H_B * NOPE_B:])
    qlat = _dot(qn, wuk_ref[...])
    qr = _rope(qr, cb, s1b, s2b, ROPE_B // 2)
    ckvn = _rms(ckv, gkv_ref[...])
    kr4 = _rope(kr4, cb, s1b, s2b, ROPE_B // 2)
    mla_o[:, 0:KV_LORA] = ckvn
    mla_o[:, KV_LORA:KV_LORA + ROPE_B] = kr4[:, 0:ROPE_B]
    km_b[:, 0:KV_LORA] = ckvn.astype(BF16)
    km_b[:, KV_LORA:] = kr4.astype(BF16)
    lane = lax.broadcasted_iota(jnp.int32, (1, LANES), 1)
    for h in range(H_B):
        qm_b[h, :, 0:KV_LORA] = (qlat[:, h * KV_LORA:(h + 1) * KV_LORA] * QSCALE_B).astype(BF16)
        qm_b[h, :, KV_LORA:] = jnp.where((lane >= h * ROPE_B) & (lane < (h + 1) * ROPE_B), qr * QSCALE_B, 0.0).astype(BF16)


def _even_proj(x, g, w, tab, gq, gkv, wuq, wuk, tm):
    r, d = x.shape
    nt = tab.shape[0] // tm
    wcols = w.shape[1]
    row = lambda n: pl.BlockSpec((tm, n), lambda i: (i, 0))
    full = lambda a: pl.BlockSpec(a.shape, lambda i: (0,) * a.ndim)
    out_shape = (
        jax.ShapeDtypeStruct((r * H_A * 2, DH_A), F32), jax.ShapeDtypeStruct((r * H_A, 2 * DH_A), F32),
        jax.ShapeDtypeStruct((r, KV_LORA + ROPE_B), F32),
        jax.ShapeDtypeStruct((r, N_A), BF16), jax.ShapeDtypeStruct((r, N_A), BF16),
        jax.ShapeDtypeStruct((r, N_A), BF16),
        jax.ShapeDtypeStruct((H_B, r, MLA_W), BF16), jax.ShapeDtypeStruct((r, MLA_W), BF16))
    out_specs = (pl.BlockSpec((tm * H_A * 2, DH_A), lambda i: (i, 0)),
                 pl.BlockSpec((tm * H_A, 2 * DH_A), lambda i: (i, 0)),
                 row(KV_LORA + ROPE_B), row(N_A), row(N_A), row(N_A),
                 pl.BlockSpec((H_B, tm, MLA_W), lambda i: (0, i, 0)), row(MLA_W))
    return pl.pallas_call(
        _even_proj_kernel, grid=(r // tm,),
        in_specs=[row(d), full(g), full(w), pl.BlockSpec((tm, tab.shape[1]), lambda i: (i % nt, 0)),
                  full(gq), full(gkv), full(wuq), full(wuk)],
        out_specs=out_specs, out_shape=out_shape,
        compiler_params=_cparams(("parallel",)), name="even_proj",
    )(x, g, w, tab, gq, gkv, wuq, wuk)


def _lam(lam_ref, lam_init):
    lf = lam_ref[...]
    a = jnp.sum(lf[0:1] * lf[1:2], axis=-1, keepdims=True)
    b = jnp.sum(lf[2:3] * lf[3:4], axis=-1, keepdims=True)
    return jnp.exp(a) - jnp.exp(b) + lam_init


def _softmax_step(s, vb, m_ref, l_ref, acc_ref):
    m_prev = m_ref[...]
    m_new = jnp.maximum(m_prev, jnp.max(s, axis=-1, keepdims=True))
    alpha = jnp.exp2(m_prev - m_new)
    p = jnp.exp2(s - _rep(m_new, s.shape[1]))
    l_ref[...] = alpha * l_ref[...] + jnp.sum(p, axis=-1, keepdims=True)
    pv = _dot(p.astype(BF16), vb)
    acc_ref[...] = _rep(alpha, acc_ref.shape[1]) * acc_ref[...] + pv
    m_ref[...] = m_new


def _attn_block(qq_ref, kb, vb, rows, cols, m_ref, l_ref, acc_ref, nsplit):
    n = qq_ref.shape[0] // nsplit
    for r in range(nsplit):
        sl = slice(r * n, (r + 1) * n)
        s = _dot_nt(qq_ref[sl, :], kb)
        if cols is not None:
            s = jnp.where(cols <= rows[sl], s, NEG)
        _softmax_step(s, vb, m_ref.at[sl, :], l_ref.at[sl, :], acc_ref.at[sl, :])


def _diff_attn_kernel(q_ref, k_ref, v_ref, lam_ref, gsub_ref, o_ref, qq_ref, m_ref, l_ref, acc_ref,
                      *, tq, tk, lam_init, nsplit):
    i = pl.program_id(2)
    lane = lax.broadcasted_iota(jnp.int32, (1, LANES), 1)
    q = q_ref[...]
    zero = jnp.zeros_like(q)
    qq_ref[0:tq, :] = jnp.where(lane < DH_A, q, zero)
    qq_ref[tq:2 * tq, :] = jnp.where(lane >= DH_A, q, zero)
    m_ref[...] = jnp.full(m_ref.shape, NEG, F32)
    l_ref[...] = jnp.zeros(l_ref.shape, F32)
    acc_ref[...] = jnp.zeros(acc_ref.shape, F32)

    rows = lax.broadcasted_iota(jnp.int32, (2 * tq, 1), 0) % tq

    def block(off, cols):
        _attn_block(qq_ref, k_ref[pl.ds(off, tk), :], v_ref[pl.ds(off, tk), :], rows, cols,
                    m_ref, l_ref, acc_ref, nsplit)

    def body(j, carry):
        block(pl.multiple_of(j * tk, tk), None)
        return carry

    lax.fori_loop(0, i * (tq // tk), body, 0)
    cols = lax.broadcasted_iota(jnp.int32, (1, tk), 1)
    for d in range(tq // tk):
        block(pl.multiple_of(i * tq + d * tk, tk), cols + d * tk)

    o = acc_ref[...] / l_ref[...]
    od = o[0:tq] - _lam(lam_ref, lam_init) * o[tq:2 * tq]
    o_ref[...] = (_rms(od, gsub_ref[...]) * (1.0 - lam_init)).astype(o_ref.dtype)


def _diff_attn(qd, kd, vd, lam_vec, gsub, b, t, lam_init):
    tq = _tile(t, 1024)
    tk = _tile(tq, 1024)
    nq = t // tq
    return pl.pallas_call(
        functools.partial(_diff_attn_kernel, tq=tq, tk=tk, lam_init=lam_init, nsplit=2),
        grid=(b, H_A, nq),
        in_specs=[pl.BlockSpec((tq, LANES), lambda bb, h, i: (bb * nq + i, h)),
                  pl.BlockSpec((t, LANES), lambda bb, h, i: (bb, h)),
                  pl.BlockSpec((t, LANES), lambda bb, h, i: (bb, h)),
                  pl.BlockSpec(lam_vec.shape, lambda bb, h, i: (0, 0)),
                  pl.BlockSpec(gsub.shape, lambda bb, h, i: (0, 0))],
        out_specs=pl.BlockSpec((tq, LANES), lambda bb, h, i: (bb * nq + i, h)),
        out_shape=jax.ShapeDtypeStruct((b * t, N_A), BF16),
        scratch_shapes=[pltpu.VMEM((2 * tq, LANES), BF16), pltpu.VMEM((2 * tq, LANES), F32),
                        pltpu.VMEM((2 * tq, LANES), F32), pltpu.VMEM((2 * tq, LANES), F32)],
        compiler_params=_cparams(("parallel", "parallel", "arbitrary")), name="diff_attn",
    )(qd, kd, vd, lam_vec, gsub)


def _mla_attn_kernel(q_ref, k_ref, wuv_ref, o_ref, qq_ref, m_ref, l_ref, acc_ref, *, tq, tk, nsplit):
    i = pl.program_id(1)
    for h in range(H_B):
        qq_ref[h * tq:(h + 1) * tq, :] = q_ref[h]
    m_ref[...] = jnp.full(m_ref.shape, NEG, F32)
    l_ref[...] = jnp.zeros(l_ref.shape, F32)
    acc_ref[...] = jnp.zeros(acc_ref.shape, F32)

    rows = lax.broadcasted_iota(jnp.int32, (H_B * tq, 1), 0) % tq

    def block(off, cols):
        kb = k_ref[pl.ds(off, tk), :]
        _attn_block(qq_ref, kb, kb[:, 0:KV_LORA], rows, cols, m_ref, l_ref, acc_ref, nsplit)

    def body(j, carry):
        block(pl.multiple_of(j * tk, tk), None)
        return carry

    lax.fori_loop(0, i * (tq // tk), body, 0)
    cols = lax.broadcasted_iota(jnp.int32, (1, tk), 1)
    for d in range(tq // tk):
        block(pl.multiple_of(i * tq + d * tk, tk), cols + d * tk)

    ol = (acc_ref[...] / _rep(l_ref[...], KV_LORA)).astype(BF16)
    for h in range(H_B):
        o_ref[:, h * V_B:(h + 1) * V_B] = _dot(ol[h * tq:(h + 1) * tq], wuv_ref[h]).astype(o_ref.dtype)


def _mla_attn(qm, km, wuv, b, t):
    tq = _tile(t, 512)
    tk = _tile(tq, 512)
    nq = t // tq
    return pl.pallas_call(
        functools.partial(_mla_attn_kernel, tq=tq, tk=tk, nsplit=2),
        grid=(b, nq),
        in_specs=[pl.BlockSpec((H_B, tq, MLA_W), lambda bb, i: (0, bb * nq + i, 0)),
                  pl.BlockSpec((t, MLA_W), lambda bb, i: (bb, 0)),
                  pl.BlockSpec(wuv.shape, lambda bb, i: (0, 0, 0))],
        out_specs=pl.BlockSpec((tq, H_B * V_B), lambda bb, i: (bb * nq + i, 0)),
        out_shape=jax.ShapeDtypeStruct((b * t, H_B * V_B), BF16),
        scratch_shapes=[pltpu.VMEM((H_B * tq, MLA_W), BF16), pltpu.VMEM((H_B * tq, LANES), F32),
                        pltpu.VMEM((H_B * tq, LANES), F32), pltpu.VMEM((H_B * tq, KV_LORA), F32)],
        compiler_params=_cparams(("parallel", "arbitrary")), name="mla_attn",
    )(qm, km, wuv)


def _post_kernel(x_ref, a1_ref, a2_ref, wo_ref, g_ref, w1_ref, w2_ref, gf_ref, o_ref,
                 x1_ref, hn_ref, acc_ref, *, final):
    j = pl.program_id(1)
    half = a1_ref.shape[1]

    @pl.when(j == 0)
    def _():
        mix = (_dot(a1_ref[...].astype(BF16), wo_ref[0:half, :])
               + _dot(a2_ref[...].astype(BF16), wo_ref[half:2 * half, :]))
        x1 = x_ref[...] + mix
        x1_ref[...] = x1
        hn_ref[...] = _rms(x1, g_ref[...]).astype(BF16)
        acc_ref[...] = jnp.zeros(acc_ref.shape, F32)

    h1 = jnp.maximum(_dot(hn_ref[...], w1_ref[...]), 0.0)
    acc_ref[...] += _dot((h1 * h1).astype(BF16), w2_ref[...])

    @pl.when(j == pl.num_programs(1) - 1)
    def _():
        y = x1_ref[...] + acc_ref[...]
        if final:
            y = _rms(y, gf_ref[...])
        o_ref[...] = y


def _post(x, a1, c1, a2, c2, wo, g, w1, w2, gf, final):
    r, d = x.shape
    tm = _tile(r, 512)
    dff = w1.shape[1]
    tf = _tile(dff, 1024)
    half = d // 2
    return pl.pallas_call(
        functools.partial(_post_kernel, final=final),
        grid=(r // tm, dff // tf),
        in_specs=[pl.BlockSpec((tm, d), lambda i, j: (i, 0)),
                  pl.BlockSpec((tm, half), lambda i, j: (i, c1)),
                  pl.BlockSpec((tm, half), lambda i, j: (i, c2)),
                  pl.BlockSpec(wo.shape, lambda i, j: (0, 0)),
                  pl.BlockSpec(g.shape, lambda i, j: (0, 0)),
                  pl.BlockSpec((d, tf), lambda i, j: (0, j)),
                  pl.BlockSpec((tf, d), lambda i, j: (j, 0)),
                  pl.BlockSpec(gf.shape, lambda i, j: (0, 0))],
        out_specs=pl.BlockSpec((tm, d), lambda i, j: (i, 0)),
        out_shape=jax.ShapeDtypeStruct((r, d), F32),
        scratch_shapes=[pltpu.VMEM((tm, d), F32), pltpu.VMEM((tm, d), BF16), pltpu.VMEM((tm, d), F32)],
        compiler_params=_cparams(("parallel", "arbitrary")), name="post_final" if final else "post",
    )(x, a1, a2, wo, g, w1, w2, gf)


NQ_C = H_C * DH_C
NK_C = KVH_C * DH_C


def _odd_proj_kernel(x_ref, g_ref, w_ref, k_o, v_o, q_b, k2_b, v2_b):
    hb = _rms(x_ref[...], g_ref[...]).astype(BF16)
    o = 0
    q_b[...] = (_dot(hb, w_ref[:, o:o + NQ_C]) * QSCALE_C).astype(BF16)
    o += NQ_C
    _store_row_groups(k_o, _dot(hb, w_ref[:, o:o + NK_C]), DH_C)
    o += NK_C
    _store_row_groups(v_o, _dot(hb, w_ref[:, o:o + NK_C]), DH_C)
    o += NK_C
    k2_b[...] = _dot(hb, w_ref[:, o:o + 2 * NK_C]).astype(BF16)
    o += 2 * NK_C
    v2_b[...] = _dot(hb, w_ref[:, o:o + 2 * NK_C]).astype(BF16)


def _odd_proj(x, g, w, tm):
    r, d = x.shape
    row = lambda n: pl.BlockSpec((tm, n), lambda i: (i, 0))
    full = lambda a: pl.BlockSpec(a.shape, lambda i: (0,) * a.ndim)
    return pl.pallas_call(
        _odd_proj_kernel, grid=(r // tm,),
        in_specs=[row(d), full(g), full(w)],
        out_specs=(pl.BlockSpec((tm * KVH_C, DH_C), lambda i: (i, 0)),
                   pl.BlockSpec((tm * KVH_C, DH_C), lambda i: (i, 0)),
                   row(NQ_C), row(2 * NK_C), row(2 * NK_C)),
        out_shape=(jax.ShapeDtypeStruct((r * KVH_C, DH_C), F32), jax.ShapeDtypeStruct((r * KVH_C, DH_C), F32),
                   jax.ShapeDtypeStruct((r, NQ_C), BF16), jax.ShapeDtypeStruct((r, 2 * NK_C), BF16),
                   jax.ShapeDtypeStruct((r, 2 * NK_C), BF16)),
        compiler_params=_cparams(("parallel",)), name="odd_proj",
    )(x, g, w)


SB_DONE = -150.0


def _sb_live(c_ref):
    return jnp.max(c_ref[...]) > SB_DONE


def _sb_step(z, vb, u, mask, c_ref, acc_ref, v_feature_major=False):
    tk = z.shape[1]
    ls = jnp.minimum(z, 0.0) - jnp.log2(1.0 + jnp.exp2(-jnp.abs(z)))
    lneg = ls - z
    if mask is not None:
        lneg = jnp.where(mask, lneg, 0.0)
    hi = lneg.astype(BF16)
    lo = (lneg - hi.astype(F32)).astype(BF16)
    suf = _dot(hi, u) + _dot(lo, u)
    c = c_ref[...]
    a = jnp.exp2(ls + suf + _rep(c, tk))
    if mask is not None:
        a = jnp.where(mask, a, 0.0)
    ab = a.astype(BF16)
    acc_ref[...] += _dot_nt(ab, vb) if v_feature_major else _dot(ab, vb)
    c_ref[...] = c + (suf[:, 0:1] + lneg[:, 0:1])


def _sb_attn_kernel(q_ref, k_ref, v_ref, u_ref, o_ref, qq_ref, c_ref, acc_ref, *, tq):
    i = pl.program_id(2)
    lane = lax.broadcasted_iota(jnp.int32, (1, LANES), 1)
    for g in range(G_C):
        qb = q_ref[:, (g // 2) * LANES:(g // 2 + 1) * LANES]
        keep = (lane < DH_C) if g % 2 == 0 else (lane >= DH_C)
        qq_ref[g * tq:(g + 1) * tq, :] = jnp.where(keep, qb, jnp.zeros_like(qb))
    c_ref[...] = jnp.zeros(c_ref.shape, F32)
    acc_ref[...] = jnp.zeros(acc_ref.shape, F32)

    rows = lax.broadcasted_iota(jnp.int32, (G_C * tq, 1), 0) % tq

    def block(off, mask):
        z = _dot_nt(qq_ref[...], k_ref[pl.ds(off, tq), :])
        _sb_step(z, v_ref[pl.ds(off, tq), :], u_ref[...], mask, c_ref, acc_ref)

    block(pl.multiple_of(i * tq, tq), lax.broadcasted_iota(jnp.int32, (1, tq), 1) < rows)

    @pl.when(i > 0)
    def _():
        block(pl.multiple_of((i - 1) * tq, tq), None)

    def cond(carry):
        j, live = carry
        return jnp.logical_and(j >= 0, live)

    def body(carry):
        j, _ = carry
        block(pl.multiple_of(j * tq, tq), None)
        return j - 1, _sb_live(c_ref)

    lax.while_loop(cond, body, (i - 2, _sb_live(c_ref)))
    acc = acc_ref[...]
    for c2 in range(G_C // 2):
        o_ref[:, c2 * LANES:(c2 + 1) * LANES] = jnp.where(
            lane < DH_C, acc[(2 * c2) * tq:(2 * c2 + 1) * tq], acc[(2 * c2 + 1) * tq:(2 * c2 + 2) * tq]
        ).astype(o_ref.dtype)


def _tri(tk):
    j = jnp.arange(tk)
    return (j[:, None] > j[None, :]).astype(BF16)


def _sb_attn(q, k2, v2, b, t):
    tq = _tile(t, 256)
    nq = t // tq
    u = _tri(tq)
    return pl.pallas_call(
        functools.partial(_sb_attn_kernel, tq=tq),
        grid=(b, KVH_C, nq),
        in_specs=[pl.BlockSpec((tq, G_C * DH_C), lambda bb, h, i: (bb * nq + i, h)),
                  pl.BlockSpec((t, LANES), lambda bb, h, i: (bb, h)),
                  pl.BlockSpec((t, LANES), lambda bb, h, i: (bb, h)),
                  pl.BlockSpec(u.shape, lambda bb, h, i: (0, 0))],
        out_specs=pl.BlockSpec((tq, G_C * DH_C), lambda bb, h, i: (bb * nq + i, h)),
        out_shape=jax.ShapeDtypeStruct((b * t, NQ_C), BF16),
        scratch_shapes=[pltpu.VMEM((G_C * tq, LANES), BF16), pltpu.VMEM((G_C * tq, LANES), F32),
                        pltpu.VMEM((G_C * tq, LANES), F32)],
        compiler_params=_cparams(("parallel", "parallel", "arbitrary")), name="sb_attn",
    )(q, k2, v2, u)


def _dec_even_kernel(pt_ref, qd_ref, ql_ref, qr_ref, ok_ref, ov_ref, om_ref, lam_ref, gsub_ref, *rest,
                     npg, ts, lam_init):
    kp = rest[0:npg]
    vp = rest[npg:2 * npg]
    mp = rest[2 * npg:3 * npg]
    od_ref, ol_ref = rest[3 * npg:3 * npg + 2]
    md_ref, ld_ref, accd_ref, mm_ref, lm_ref, accm_ref = rest[3 * npg + 2:]
    s_id = pl.program_id(1)
    rows_h = 2 * ts

    @pl.when(s_id == 0)
    def _():
        md_ref[...] = jnp.full(md_ref.shape, NEG, F32)
        ld_ref[...] = jnp.zeros(ld_ref.shape, F32)
        accd_ref[...] = jnp.zeros(accd_ref.shape, F32)
        mm_ref[...] = jnp.full(mm_ref.shape, NEG, F32)
        lm_ref[...] = jnp.zeros(lm_ref.shape, F32)
        accm_ref[...] = jnp.zeros(accm_ref.shape, F32)

    def softmax(s, m_ref, l_ref):
        m_prev = m_ref[...]
        m_new = jnp.maximum(m_prev, jnp.max(s, axis=-1, keepdims=True))
        alpha = jnp.exp2(m_prev - m_new)
        p = jnp.exp2(s - _rep(m_new, s.shape[1]))
        l_ref[...] = alpha * l_ref[...] + jnp.sum(p, axis=-1, keepdims=True)
        m_ref[...] = m_new
        return p.astype(BF16), alpha

    def cat(xs, axis):
        return xs[0] if len(xs) == 1 else jnp.concatenate(xs, axis=axis)

    def update(ks, vs, ms, dmask, mmask):
        sd = _dot(qd_ref[...], cat([k[...].astype(BF16) for k in ks], 1))
        if dmask is not None:
            sd = jnp.where(dmask, sd, NEG)
        pd, alpha = softmax(sd, md_ref, ld_ref)
        for h in range(H_A):
            vh = cat([v[pl.ds(h, PAGE, stride=H_A), :].astype(BF16) for v in vs], 0)
            r = slice(h * rows_h, (h + 1) * rows_h)
            accd_ref[r, :] = alpha[r] * accd_ref[r, :] + _dot(pd[r], vh)
        lat = cat([m[0:KV_LORA, :].astype(BF16) for m in ms], 1)
        rope = cat([m[KV_LORA:KV_LORA + ROPE_B, :].astype(BF16) for m in ms], 1)
        sm = _dot(ql_ref[...], lat) + _dot(qr_ref[...], rope)
        if mmask is not None:
            sm = jnp.where(mmask, sm, NEG)
        pm, alpha = softmax(sm, mm_ref, lm_ref)
        accm_ref[...] = _rep(alpha, KV_LORA) * accm_ref[...] + _dot_nt(pm, lat)

    update(kp, vp, mp, None, None)

    @pl.when(s_id == pl.num_programs(1) - 1)
    def _():
        key = lax.broadcasted_iota(jnp.int32, (1, PAGE), 1)
        tok_d = lax.broadcasted_iota(jnp.int32, (H_A * rows_h, 1), 0) % ts
        tok_m = lax.broadcasted_iota(jnp.int32, (H_B * ts, 1), 0) % ts
        update([ok_ref], [ov_ref], [om_ref], key <= tok_d, key <= tok_m)
        o = accd_ref[...] / ld_ref[...]
        lam = _lam(lam_ref, lam_init)
        for h in range(H_A):
            od = o[h * rows_h:h * rows_h + ts] - lam * o[h * rows_h + ts:(h + 1) * rows_h]
            od_ref[:, h * LANES:(h + 1) * LANES] = _rms(od, gsub_ref[...]) * (1.0 - lam_init)
        ol_ref[...] = accm_ref[...] / _rep(lm_ref[...], KV_LORA)


def _dec_even(pt, qd, ql, qr, own_k, own_v, own_m, lam_vec, gsub, ck, cv, cm, layer, npg, lam_init):
    nb, n_pages = pt.shape
    ts = qd.shape[1] // (H_A * 2)
    nsteps = n_pages // npg
    ptf = pt.reshape(-1)

    def page_spec(a, p):
        return pl.BlockSpec((None, None) + a.shape[2:],
                            lambda b, s, pt_ref: (layer, pt_ref[b * n_pages + s * npg + p], 0, 0))

    per_b = lambda a: pl.BlockSpec((None,) + a.shape[1:], lambda b, s, pt_ref: (b,) + (0,) * (a.ndim - 1))
    full = lambda a: pl.BlockSpec(a.shape, lambda b, s, pt_ref: (0,) * a.ndim)
    in_specs = ([per_b(qd), per_b(ql), per_b(qr), per_b(own_k), per_b(own_v), per_b(own_m),
                 full(lam_vec), full(gsub)]
                + [page_spec(ck, p) for p in range(npg)]
                + [page_spec(cv, p) for p in range(npg)]
                + [page_spec(cm, p) for p in range(npg)])
    rd, rm = qd.shape[1], ql.shape[1]
    grid_spec = pltpu.PrefetchScalarGridSpec(
        num_scalar_prefetch=1, grid=(nb, nsteps), in_specs=in_specs,
        out_specs=(pl.BlockSpec((None, ts, N_A), lambda b, s, pt_ref: (b, 0, 0)),
                   pl.BlockSpec((None, rm, KV_LORA), lambda b, s, pt_ref: (b, 0, 0))),
        scratch_shapes=[pltpu.VMEM((rd, LANES), F32), pltpu.VMEM((rd, LANES), F32), pltpu.VMEM((rd, LANES), F32),
                        pltpu.VMEM((rm, LANES), F32), pltpu.VMEM((rm, LANES), F32), pltpu.VMEM((rm, KV_LORA), F32)])
    return pl.pallas_call(
        functools.partial(_dec_even_kernel, npg=npg, ts=ts, lam_init=lam_init),
        grid_spec=grid_spec,
        out_shape=(jax.ShapeDtypeStruct((nb, ts, N_A), F32), jax.ShapeDtypeStruct((nb, rm, KV_LORA), F32)),
        compiler_params=_cparams(("parallel", "arbitrary")), name="dec_even",
    )(ptf, qd, ql, qr, own_k, own_v, own_m, lam_vec, gsub, *([ck] * npg), *([cv] * npg), *([cm] * npg))


def _uv_kernel(ol_ref, wuv_ref, o_ref):
    o_ref[...] = _dot(ol_ref[...].astype(BF16), wuv_ref[...]).astype(o_ref.dtype)


def _uv(ol, wuv):
    _, r, _ = ol.shape
    return pl.pallas_call(
        _uv_kernel, grid=(H_B,),
        in_specs=[pl.BlockSpec((None, r, KV_LORA), lambda h: (h, 0, 0)),
                  pl.BlockSpec((None, KV_LORA, V_B), lambda h: (h, 0, 0))],
        out_specs=pl.BlockSpec((r, V_B), lambda h: (0, h)),
        out_shape=jax.ShapeDtypeStruct((r, H_B * V_B), BF16),
        compiler_params=_cparams(("parallel",)), name="mla_uv",
    )(ol, wuv)


def _dec_sb_kernel(pt_ref, q_ref, u_ref, *rest, npg, ts, first):
    prev = rest[0:2]
    kp = rest[2:2 + npg]
    vp = rest[2 + npg:2 + 2 * npg]
    c_ref, acc_ref = rest[2 + 2 * npg:]
    s_id = pl.program_id(1)

    def block(pages, mask):
        cat = lambda xs: xs[0] if len(xs) == 1 else jnp.concatenate(xs, axis=1)
        n = len(pages) * PAGE
        z = _dot(q_ref[...], cat([k[...].astype(BF16) for k, _ in pages]))
        _sb_step(z, cat([v[...].astype(BF16) for _, v in pages]), u_ref[0:n, 0:n], mask, c_ref, acc_ref,
                 v_feature_major=True)

    todo = list(zip(kp, vp))
    if first:
        c_ref[...] = jnp.zeros(c_ref.shape, F32)
        acc_ref[...] = jnp.zeros(acc_ref.shape, F32)
        key = lax.broadcasted_iota(jnp.int32, (1, 2 * PAGE), 1)
        tok = lax.broadcasted_iota(jnp.int32, (H_C * ts, 1), 0) % ts
        block([todo[0], (prev[0], prev[1])], key < tok + PAGE)
        todo = todo[1:]
    else:
        @pl.when(s_id == 0)
        def _():
            c_ref[...] = prev[0][...]
            acc_ref[...] = prev[1][...]

    @pl.when(_sb_live(c_ref))
    def _():
        for p in range(0, len(todo), 2):
            @pl.when(_sb_live(c_ref))
            def _():
                block(todo[p:p + 2][::-1], None)


def _dec_sb(pt, q, prev, ck, cv, layer, npg, hi, nsteps, first):
    nb, n_pages = pt.shape
    ts = q.shape[1] // H_C
    ptf = pt.reshape(-1)
    w = ck.shape[-2]
    assert nsteps == 1 or not first
    u = _tri(2 * PAGE)

    def page_spec(p):
        return pl.BlockSpec((None, None, w, PAGE),
                            lambda b, s, pt_ref: (layer, pt_ref[b * n_pages + hi - 1 - (s * npg + p)], 0, 0))

    per_b = lambda a: pl.BlockSpec((None,) + a.shape[1:], lambda b, s, pt_ref: (b,) + (0,) * (a.ndim - 1))
    in_specs = ([per_b(q), pl.BlockSpec(u.shape, lambda b, s, pt_ref: (0, 0)), per_b(prev[0]), per_b(prev[1])]
                + [page_spec(p) for p in range(npg)] + [page_spec(p) for p in range(npg)])
    rq = q.shape[1]
    grid_spec = pltpu.PrefetchScalarGridSpec(
        num_scalar_prefetch=1, grid=(nb, nsteps), in_specs=in_specs,
        out_specs=(pl.BlockSpec((None, rq, LANES), lambda b, s, pt_ref: (b, 0, 0)),
                   pl.BlockSpec((None, rq, w), lambda b, s, pt_ref: (b, 0, 0))))
    return pl.pallas_call(
        functools.partial(_dec_sb_kernel, npg=npg, ts=ts, first=first),
        grid_spec=grid_spec,
        out_shape=(jax.ShapeDtypeStruct((nb, rq, LANES), F32), jax.ShapeDtypeStruct((nb, rq, w), F32)),
        compiler_params=_cparams(("parallel", "arbitrary")), name="dec_sb_first" if first else "dec_sb_rest",
    )(ptf, q, u, prev[0], prev[1], *([ck] * npg), *([cv] * npg))


def _dec_sb_all(pt, q, own_k, own_v, ck, cv, layer, npg):
    n_pages = pt.shape[1]
    state = _dec_sb(pt, q, (own_k, own_v), ck, cv, layer, npg, n_pages, 1, True)
    if n_pages > npg:
        rest = lambda st: _dec_sb(pt, q, st, ck, cv, layer, npg, n_pages - npg, n_pages // npg - 1, False)
        state = lax.cond(jnp.max(state[0]) > SB_DONE, rest, lambda st: st, state)
    return state[1]


def _pad_rows(a, n):
    return jnp.pad(a, ((0, 0), (0, n - a.shape[1]), (0, 0)))


def _feature_major(a, n):
    at = jnp.swapaxes(a.astype(BF16), 1, 2)
    return jnp.pad(at, ((0, 0), (0, 0), (0, n - at.shape[2])))


def _block_diag_queries(q, groups, width):
    nb, ts, _ = q.shape
    eye = jnp.eye(groups, dtype=q.dtype)
    qg = q.reshape(nb, ts, groups, width)
    out = jnp.einsum('btgw,hg->bhtgw', qg, eye)
    return out.reshape(nb, groups * ts, groups * width)


def kernel(x_prompt, x_sample, cache_diff_k, cache_diff_v, cache_mla, cache_sb_k, cache_sb_v, page_table,
           g_mix, g_ffn, w_in_even, diff_lambda, g_diff_sub, g_mla_q, g_mla_kv, w_mla_uq, w_mla_uk, w_mla_uv,
           w_out_even, w_in_odd, w_out_odd, w_ff1, w_ff2, g_final):
    b, t, d = x_prompt.shape
    nb, ts, _ = x_sample.shape
    n_pages = page_table.shape[1]
    past = n_pages * PAGE
    depth = g_mix.shape[0]
    n_even, n_odd = (depth + 1) // 2, depth // 2
    n_pool = cache_diff_k.shape[1]
    npg = _tile(n_pages, 8)
    npg_e = _tile(n_pages, 16)

    xp = x_prompt.reshape(b * t, d)
    xs = x_sample.reshape(nb * ts, d)
    tm_p = _tile(b * t, 512)
    tm_s = _tile(nb * ts, 512)
    tab_p = _rope_tables(jnp.arange(t))
    tab_s = jnp.tile(_rope_tables(past + jnp.arange(ts)), (tm_s // ts, 1))

    ck = jnp.transpose(cache_diff_k, (0, 1, 3, 4, 5, 2)).reshape(n_even, n_pool, N_A, PAGE)
    cv = cache_diff_v.reshape(n_even, n_pool, PAGE * H_A, 2 * DH_A)
    cm = jnp.swapaxes(cache_mla, 2, 3)
    csk = jnp.transpose(cache_sb_k, (0, 1, 3, 4, 2)).reshape(n_odd, n_pool, NK_C, PAGE)
    csv = jnp.transpose(cache_sb_v, (0, 1, 3, 4, 2)).reshape(n_odd, n_pool, NK_C, PAGE)

    row2 = lambda v: v.reshape(1, -1)
    outs_p = {k: [] for k in ("dk", "dv", "ml", "sk", "sv")}
    outs_s = {k: [] for k in ("dk", "dv", "ml", "sk", "sv")}

    for l in range(depth):
        w1 = w_ff1[l].astype(BF16)
        w2 = w_ff2[l].astype(BF16)
        final = l == depth - 1
        gf = row2(g_final)
        if l % 2 == 0:
            e = l // 2
            lam_init = 0.8 - 0.6 * math.exp(-0.3 * l)
            wi = w_in_even[e]
            c0 = 3 * N_A + Q_LORA + KV_LORA
            w_all = jnp.concatenate([wi[:, :c0]] + [wi[:, c0:]] * H_B, axis=1).astype(BF16)
            uq = w_mla_uq[e].reshape(Q_LORA, H_B, NOPE_B + ROPE_B)
            wuq = jnp.concatenate([uq[:, :, :NOPE_B].reshape(Q_LORA, H_B * NOPE_B),
                                   uq[:, :, NOPE_B:].reshape(Q_LORA, H_B * ROPE_B)], axis=1).astype(BF16)
            wuk = jnp.einsum('hcn,hg->hngc', w_mla_uk[e], jnp.eye(H_B, dtype=F32)).reshape(
                H_B * NOPE_B, H_B * KV_LORA).astype(BF16)
            wuv = w_mla_uv[e].astype(BF16)
            wo = w_out_even[e].astype(BF16)
            gq, gkv, gsub = row2(g_mla_q[e]), row2(g_mla_kv[e]), row2(g_diff_sub[e])
            lam_vec = diff_lambda[e]
            gm = row2(g_mix[l])

            kd, vd, ml, qd_b, kd_b, vd_b, qm_b, km_b = _even_proj(xp, gm, w_all, tab_p, gq, gkv, wuq, wuk, tm_p)
            outs_p["dk"].append(kd.reshape(b, t, H_A, 2, DH_A))
            outs_p["dv"].append(vd.reshape(b, t, H_A, 2 * DH_A))
            outs_p["ml"].append(ml.reshape(b, t, KV_LORA + ROPE_B))
            od = _diff_attn(qd_b, kd_b, vd_b, lam_vec, gsub, b, t, lam_init)
            om = _mla_attn(qm_b, km_b, wuv, b, t)
            xp = _post(xp, od, 0, om, 0, wo, row2(g_ffn[l]), w1, w2, gf, final)

            kd, vd, ml, qd_b, _, _, qm_b, _ = _even_proj(xs, gm, w_all, tab_s, gq, gkv, wuq, wuk, tm_s)
            outs_s["dk"].append(kd.reshape(nb, ts, H_A, 2, DH_A))
            outs_s["dv"].append(vd.reshape(nb, ts, H_A, 2 * DH_A))
            outs_s["ml"].append(ml.reshape(nb, ts, KV_LORA + ROPE_B))
            qbd = _block_diag_queries(qd_b.reshape(nb, ts, N_A), H_A * 2, DH_A)
            qm4 = qm_b.reshape(H_B, nb, ts, MLA_W)
            ql = jnp.transpose(qm4[..., :KV_LORA], (1, 0, 2, 3)).reshape(nb, H_B * ts, KV_LORA)
            qr = jnp.stack([qm4[h, :, :, KV_LORA + h * ROPE_B:KV_LORA + (h + 1) * ROPE_B] for h in range(H_B)],
                           axis=1).reshape(nb, H_B * ts, ROPE_B)
            od_s, ol_s = _dec_even(page_table, qbd, ql, qr,
                                   _feature_major(kd.reshape(nb, ts, N_A), PAGE),
                                   _pad_rows(vd.reshape(nb, ts * H_A, 2 * DH_A), PAGE * H_A),
                                   _feature_major(ml.reshape(nb, ts, KV_LORA + ROPE_B), PAGE),
                                   lam_vec, gsub, ck, cv, cm, e, npg_e, lam_init)
            ol_h = jnp.transpose(ol_s.reshape(nb, H_B, ts, KV_LORA), (1, 0, 2, 3)).reshape(H_B, nb * ts, KV_LORA)
            om_s = _uv(ol_h, wuv)
            xs = _post(xs, od_s.reshape(nb * ts, N_A), 0, om_s, 0, wo, row2(g_ffn[l]), w1, w2, gf, final)
        else:
            o = l // 2
            wi = w_in_odd[o]
            wq, wk, wv = wi[:, :NQ_C], wi[:, NQ_C:NQ_C + NK_C], wi[:, NQ_C + NK_C:]
            dup = lambda w: jnp.repeat(w.reshape(d, KVH_C, 1, DH_C), 2, axis=2).reshape(d, 2 * NK_C)
            w_all = jnp.concatenate([wq, wk, wv, dup(wk), dup(wv)], axis=1).astype(BF16)
            wo = w_out_odd[o].astype(BF16)
            gm = row2(g_mix[l])

            k, v, q_b, k2_b, v2_b = _odd_proj(xp, gm, w_all, tm_p)
            outs_p["sk"].append(k.reshape(b, t, KVH_C, DH_C))
            outs_p["sv"].append(v.reshape(b, t, KVH_C, DH_C))
            att = _sb_attn(q_b, k2_b, v2_b, b, t)
            xp = _post(xp, att, 0, att, 1, wo, row2(g_ffn[l]), w1, w2, gf, final)

            k, v, q_b, _, _ = _odd_proj(xs, gm, w_all, tm_s)
            outs_s["sk"].append(k.reshape(nb, ts, KVH_C, DH_C))
            outs_s["sv"].append(v.reshape(nb, ts, KVH_C, DH_C))
            qg = jnp.transpose(q_b.reshape(nb, ts, KVH_C, G_C, DH_C), (0, 2, 3, 1, 4))
            qbd = jnp.einsum('bkgtd,kj->bkgtjd', qg, jnp.eye(KVH_C, dtype=BF16)).reshape(
                nb, H_C * ts, NK_C)
            acc_s = _dec_sb_all(page_table, qbd, _feature_major(k.reshape(nb, ts, NK_C), PAGE),
                                _feature_major(v.reshape(nb, ts, NK_C), PAGE), csk, csv, o, npg)
            acc5 = acc_s.reshape(nb, KVH_C, G_C, ts, KVH_C, DH_C)
            att_s = jnp.stack([acc5[:, kh, :, :, kh, :] for kh in range(KVH_C)], axis=1)
            att_s = jnp.transpose(att_s, (0, 3, 1, 2, 4)).reshape(nb * ts, NQ_C)
            xs = _post(xs, att_s, 0, att_s, 1, wo, row2(g_ffn[l]), w1, w2, gf, final)

    st = lambda xs_: jnp.stack(xs_)
    return (xp.reshape(b, t, d), xs.reshape(nb, ts, d),
            st(outs_p["dk"]), st(outs_p["dv"]), st(outs_p["ml"]), st(outs_p["sk"]), st(outs_p["sv"]),
            st(outs_s["dk"]), st(outs_s["dv"]), st(outs_s["ml"]), st(outs_s["sk"]), st(outs_s["sv"]))
```

```python
import functools
import math

import jax
import jax.numpy as jnp
from jax import lax
from jax.experimental import pallas as pl
from jax.experimental.pallas import tpu as pltpu

F32 = jnp.float32
BF16 = jnp.bfloat16

EPS = 1e-6
NEG = -1e30
H_A = 4
DH_A = 64
ROT_A = DH_A // 4
ROPE_THETA = 500000.0
SCALE_A = DH_A ** -0.5
H_B = 4
Q_LORA = 384
KV_LORA = 256
NOPE_B = 64
ROPE_B = 32
V_B = 128
MLA_THETA = 10000.0
SCALE_B = (NOPE_B + ROPE_B) ** -0.5
H_C = 16
KVH_C = 4
G_C = H_C // KVH_C
DH_C = 64
SCALE_C = DH_C ** -0.5
PAGE = 128
LOG2E = math.log2(math.e)
QSCALE_A = SCALE_A * LOG2E
QSCALE_B = SCALE_B * LOG2E
QSCALE_C = SCALE_C * LOG2E

LANES = 128
VMEM_LIMIT = 48 * 1024 * 1024

N_A = H_A * 2 * DH_A
MLA_W = KV_LORA + LANES


def _tile(n, pref):
    t = min(n, pref)
    while n % t:
        t //= 2
    return t


def _dot(a, b):
    return jnp.dot(a, b, preferred_element_type=F32)


def _dot_nt(a, b):
    return lax.dot_general(a, b, (((1,), (1,)), ((), ())), preferred_element_type=F32)


def _rep(x, width):
    n = width // LANES
    return x if n == 1 else pltpu.repeat(x, n, 1)


def _store_row_groups(o_ref, y, width):
    tm = y.shape[0]
    g = y.shape[1] // width
    for j in range(g):
        o_ref[pl.ds(j, tm, stride=g), :] = y[:, j * width:(j + 1) * width]


def _rms(x, g):
    return x * lax.rsqrt(jnp.mean(x * x, axis=-1, keepdims=True) + EPS) * g


def _cparams(sem, vmem=VMEM_LIMIT):
    return pltpu.CompilerParams(dimension_semantics=sem, vmem_limit_bytes=vmem)


def _rope_tables(pos):
    posf = pos.astype(F32)[:, None]
    lane = jnp.arange(LANES)

    def tab(period, half, theta):
        inv = theta ** (-jnp.arange(half, dtype=F32) / half)
        ang = posf * inv[None, :]
        cos, sin = jnp.cos(ang), jnp.sin(ang)
        d = lane % period
        first = d < half
        second = (d >= half) & (d < 2 * half)
        idx = jnp.where(first, d, jnp.where(second, d - half, 0))
        c = jnp.where((first | second)[None, :], cos[:, idx], 1.0)
        s1 = jnp.where(first[None, :], -sin[:, idx], 0.0)
        s2 = jnp.where(second[None, :], sin[:, idx], 0.0)
        return [c, s1, s2]

    return jnp.concatenate(tab(DH_A, ROT_A // 2, ROPE_THETA) + tab(ROPE_B, ROPE_B // 2, MLA_THETA), axis=1)


def _rope(y, c, s1, s2, half):
    outs = []
    for k in range(y.shape[1] // LANES):
        yb = y[:, k * LANES:(k + 1) * LANES]
        outs.append(yb * c + pltpu.roll(yb, LANES - half, 1) * s1 + pltpu.roll(yb, half, 1) * s2)
    return outs[0] if len(outs) == 1 else jnp.concatenate(outs, axis=1)


def _even_proj_kernel(x_ref, g_ref, w_ref, tab_ref, gq_ref, gkv_ref, wuq_ref, wuk_ref,
                      kd_o, vd_o, mla_o, qd_b, kd_b, vd_b, qm_b, km_b):
    hb = _rms(x_ref[...], g_ref[...]).astype(BF16)
    tab = tab_ref[...]
    ca, s1a, s2a, cb, s1b, s2b = [tab[:, k * LANES:(k + 1) * LANES] for k in range(6)]
    o = 0
    qd = _rope(_dot(hb, w_ref[:, o:o + N_A]), ca, s1a, s2a, ROT_A // 2)
    qd_b[...] = (qd * QSCALE_A).astype(BF16)
    o += N_A
    kd = _rope(_dot(hb, w_ref[:, o:o + N_A]), ca, s1a, s2a, ROT_A // 2)
    _store_row_groups(kd_o, kd, DH_A)
    kd_b[...] = kd.astype(BF16)
    o += N_A
    vd = _dot(hb, w_ref[:, o:o + N_A])
    _store_row_groups(vd_o, vd, 2 * DH_A)
    vd_b[...] = vd.astype(BF16)
    o += N_A
    cq = _dot(hb, w_ref[:, o:o + Q_LORA])
    o += Q_LORA
    ckv = _dot(hb, w_ref[:, o:o + KV_LORA])
    o += KV_LORA
    kr4 = _dot(hb, w_ref[:, o:o + LANES])
    cqn = _rms(cq, gq_ref[...]).astype(BF16)
    qn = _dot(cqn, wuq_ref[:, 0:H_B * NOPE_B]).astype(BF16)
    qr = _dot(cqn, wuq_ref[:, H_B * NOPE_B:])
    qlat = _dot(qn, wuk_ref[...])
    qr = _rope(qr, cb, s1b, s2b, ROPE_B // 2)
    ckvn = _rms(ckv, gkv_ref[...])
    kr4 = _rope(kr4, cb, s1b, s2b, ROPE_B // 2)
    mla_o[:, 0:KV_LORA] = ckvn
    mla_o[:, KV_LORA:KV_LORA + ROPE_B] = kr4[:, 0:ROPE_B]
    km_b[:, 0:KV_LORA] = ckvn.astype(BF16)
    km_b[:, KV_LORA:] = kr4.astype(BF16)
    lane = lax.broadcasted_iota(jnp.int32, (1, LANES), 1)
    for h in range(H_B):
        qm_b[h, :, 0:KV_LORA] = (qlat[:, h * KV_LORA:(h + 1) * KV_LORA] * QSCALE_B).astype(BF16)
        qm_b[h, :, KV_LORA:] = jnp.where((lane >= h * ROPE_B) & (lane < (h + 1) * ROPE_B), qr * QSCALE_B, 0.0).astype(BF16)


def _even_proj(x, g, w, tab, gq, gkv, wuq, wuk, tm):
    r, d = x.shape
    nt = tab.shape[0] // tm
    wcols = w.shape[1]
    row = lambda n: pl.BlockSpec((tm, n), lambda i: (i, 0))
    full = lambda a: pl.BlockSpec(a.shape, lambda i: (0,) * a.ndim)
    out_shape = (
        jax.ShapeDtypeStruct((r * H_A * 2, DH_A), F32), jax.ShapeDtypeStruct((r * H_A, 2 * DH_A), F32),
        jax.ShapeDtypeStruct((r, KV_LORA + ROPE_B), F32),
        jax.ShapeDtypeStruct((r, N_A), BF16), jax.ShapeDtypeStruct((r, N_A), BF16),
        jax.ShapeDtypeStruct((r, N_A), BF16),
        jax.ShapeDtypeStruct((H_B, r, MLA_W), BF16), jax.ShapeDtypeStruct((r, MLA_W), BF16))
    out_specs = (pl.BlockSpec((tm * H_A * 2, DH_A), lambda i: (i, 0)),
                 pl.BlockSpec((tm * H_A, 2 * DH_A), lambda i: (i, 0)),
                 row(KV_LORA + ROPE_B), row(N_A), row(N_A), row(N_A),
                 pl.BlockSpec((H_B, tm, MLA_W), lambda i: (0, i, 0)), row(MLA_W))
    return pl.pallas_call(
        _even_proj_kernel, grid=(r // tm,),
        in_specs=[row(d), full(g), full(w), pl.BlockSpec((tm, tab.shape[1]), lambda i: (i % nt, 0)),
                  full(gq), full(gkv), full(wuq), full(wuk)],
        out_specs=out_specs, out_shape=out_shape,
        compiler_params=_cparams(("parallel",)), name="even_proj",
    )(x, g, w, tab, gq, gkv, wuq, wuk)


def _lam(lam_ref, lam_init):
    lf = lam_ref[...]
    a = jnp.sum(lf[0:1] * lf[1:2], axis=-1, keepdims=True)
    b = jnp.sum(lf[2:3] * lf[3:4], axis=-1, keepdims=True)
    return jnp.exp(a) - jnp.exp(b) + lam_init


def _softmax_step(s, vb, m_ref, l_ref, acc_ref):
    m_prev = m_ref[...]
    m_new = jnp.maximum(m_prev, jnp.max(s, axis=-1, keepdims=True))
    alpha = jnp.exp2(m_prev - m_new)
    p = jnp.exp2(s - _rep(m_new, s.shape[1]))
    l_ref[...] = alpha * l_ref[...] + jnp.sum(p, axis=-1, keepdims=True)
    pv = _dot(p.astype(BF16), vb)
    acc_ref[...] = _rep(alpha, acc_ref.shape[1]) * acc_ref[...] + pv
    m_ref[...] = m_new


def _attn_block(qq_ref, kb, vb, rows, cols, m_ref, l_ref, acc_ref, nsplit):
    n = qq_ref.shape[0] // nsplit
    for r in range(nsplit):
        sl = slice(r * n, (r + 1) * n)
        s = _dot_nt(qq_ref[sl, :], kb)
        if cols is not None:
            s = jnp.where(cols <= rows[sl], s, NEG)
        _softmax_step(s, vb, m_ref.at[sl, :], l_ref.at[sl, :], acc_ref.at[sl, :])


def _diff_attn_kernel(q_ref, k_ref, v_ref, lam_ref, gsub_ref, o_ref, qq_ref, m_ref, l_ref, acc_ref,
                      *, tq, tk, lam_init, nsplit):
    i = pl.program_id(2)
    lane = lax.broadcasted_iota(jnp.int32, (1, LANES), 1)
    q = q_ref[...]
    zero = jnp.zeros_like(q)
    qq_ref[0:tq, :] = jnp.where(lane < DH_A, q, zero)
    qq_ref[tq:2 * tq, :] = jnp.where(lane >= DH_A, q, zero)
    m_ref[...] = jnp.full(m_ref.shape, NEG, F32)
    l_ref[...] = jnp.zeros(l_ref.shape, F32)
    acc_ref[...] = jnp.zeros(acc_ref.shape, F32)

    rows = lax.broadcasted_iota(jnp.int32, (2 * tq, 1), 0) % tq

    def block(off, cols):
        _attn_block(qq_ref, k_ref[pl.ds(off, tk), :], v_ref[pl.ds(off, tk), :], rows, cols,
                    m_ref, l_ref, acc_ref, nsplit)

    def body(j, carry):
        block(pl.multiple_of(j * tk, tk), None)
        return carry

    lax.fori_loop(0, i * (tq // tk), body, 0)
    cols = lax.broadcasted_iota(jnp.int32, (1, tk), 1)
    for d in range(tq // tk):
        block(pl.multiple_of(i * tq + d * tk, tk), cols + d * tk)

    o = acc_ref[...] / l_ref[...]
    od = o[0:tq] - _lam(lam_ref, lam_init) * o[tq:2 * tq]
    o_ref[...] = (_rms(od, gsub_ref[...]) * (1.0 - lam_init)).astype(o_ref.dtype)


def _diff_attn(qd, kd, vd, lam_vec, gsub, b, t, lam_init):
    tq = _tile(t, 1024)
    tk = _tile(tq, 1024)
    nq = t // tq
    return pl.pallas_call(
        functools.partial(_diff_attn_kernel, tq=tq, tk=tk, lam_init=lam_init, nsplit=2),
        grid=(b, H_A, nq),
        in_specs=[pl.BlockSpec((tq, LANES), lambda bb, h, i: (bb * nq + i, h)),
                  pl.BlockSpec((t, LANES), lambda bb, h, i: (bb, h)),
                  pl.BlockSpec((t, LANES), lambda bb, h, i: (bb, h)),
                  pl.BlockSpec(lam_vec.shape, lambda bb, h, i: (0, 0)),
                  pl.BlockSpec(gsub.shape, lambda bb, h, i: (0, 0))],
        out_specs=pl.BlockSpec((tq, LANES), lambda bb, h, i: (bb * nq + i, h)),
        out_shape=jax.ShapeDtypeStruct((b * t, N_A), BF16),
        scratch_shapes=[pltpu.VMEM((2 * tq, LANES), BF16), pltpu.VMEM((2 * tq, LANES), F32),
                        pltpu.VMEM((2 * tq, LANES), F32), pltpu.VMEM((2 * tq, LANES), F32)],
        compiler_params=_cparams(("parallel", "parallel", "arbitrary")), name="diff_attn",
    )(qd, kd, vd, lam_vec, gsub)


def _mla_attn_kernel(q_ref, k_ref, wuv_ref, o_ref, qq_ref, m_ref, l_ref, acc_ref, *, tq, tk, nsplit):
    i = pl.program_id(1)
    for h in range(H_B):
        qq_ref[h * tq:(h + 1) * tq, :] = q_ref[h]
    m_ref[...] = jnp.full(m_ref.shape, NEG, F32)
    l_ref[...] = jnp.zeros(l_ref.shape, F32)
    acc_ref[...] = jnp.zeros(acc_ref.shape, F32)

    rows = lax.broadcasted_iota(jnp.int32, (H_B * tq, 1), 0) % tq

    def block(off, cols):
        kb = k_ref[pl.ds(off, tk), :]
        _attn_block(qq_ref, kb, kb[:, 0:KV_LORA], rows, cols, m_ref, l_ref, acc_ref, nsplit)

    def body(j, carry):
        block(pl.multiple_of(j * tk, tk), None)
        return carry

    lax.fori_loop(0, i * (tq // tk), body, 0)
    cols = lax.broadcasted_iota(jnp.int32, (1, tk), 1)
    for d in range(tq // tk):
        block(pl.multiple_of(i * tq + d * tk, tk), cols + d * tk)

    ol = (acc_ref[...] / _rep(l_ref[...], KV_LORA)).astype(BF16)
    for h in range(H_B):
        o_ref[:, h * V_B:(h + 1) * V_B] = _dot(ol[h * tq:(h + 1) * tq], wuv_ref[h]).astype(o_ref.dtype)


def _mla_attn(qm, km, wuv, b, t):
    tq = _tile(t, 512)
    tk = _tile(tq, 512)
    nq = t // tq
    return pl.pallas_call(
        functools.partial(_mla_attn_kernel, tq=tq, tk=tk, nsplit=2),
        grid=(b, nq),
        in_specs=[pl.BlockSpec((H_B, tq, MLA_W), lambda bb, i: (0, bb * nq + i, 0)),
                  pl.BlockSpec((t, MLA_W), lambda bb, i: (bb, 0)),
                  pl.BlockSpec(wuv.shape, lambda bb, i: (0, 0, 0))],
        out_specs=pl.BlockSpec((tq, H_B * V_B), lambda bb, i: (bb * nq + i, 0)),
        out_shape=jax.ShapeDtypeStruct((b * t, H_B * V_B), BF16),
        scratch_shapes=[pltpu.VMEM((H_B * tq, MLA_W), BF16), pltpu.VMEM((H_B * tq, LANES), F32),
                        pltpu.VMEM((H_B * tq, LANES), F32), pltpu.VMEM((H_B * tq, KV_LORA), F32)],
        compiler_params=_cparams(("parallel", "arbitrary")), name="mla_attn",
    )(qm, km, wuv)


def _post_kernel(x_ref, a1_ref, a2_ref, wo_ref, g_ref, w1_ref, w2_ref, gf_ref, o_ref,
                 x1_ref, hn_ref, acc_ref, *, final):
    j = pl.program_id(1)
    half = a1_ref.shape[1]

    @pl.when(j == 0)
    def _():
        mix = (_dot(a1_ref[...].astype(BF16), wo_ref[0:half, :])
               + _dot(a2_ref[...].astype(BF16), wo_ref[half:2 * half, :]))
        x1 = x_ref[...] + mix
        x1_ref[...] = x1
        hn_ref[...] = _rms(x1, g_ref[...]).astype(BF16)
        acc_ref[...] = jnp.zeros(acc_ref.shape, F32)

    h1 = jnp.maximum(_dot(hn_ref[...], w1_ref[...]), 0.0)
    acc_ref[...] += _dot((h1 * h1).astype(BF16), w2_ref[...])

    @pl.when(j == pl.num_programs(1) - 1)
    def _():
        y = x1_ref[...] + acc_ref[...]
        if final:
            y = _rms(y, gf_ref[...])
        o_ref[...] = y


def _post(x, a1, c1, a2, c2, wo, g, w1, w2, gf, final):
    r, d = x.shape
    tm = _tile(r, 512)
    dff = w1.shape[1]
    tf = _tile(dff, 2048)
    half = d // 2
    return pl.pallas_call(
        functools.partial(_post_kernel, final=final),
        grid=(r // tm, dff // tf),
        in_specs=[pl.BlockSpec((tm, d), lambda i, j: (i, 0)),
                  pl.BlockSpec((tm, half), lambda i, j: (i, c1)),
                  pl.BlockSpec((tm, half), lambda i, j: (i, c2)),
                  pl.BlockSpec(wo.shape, lambda i, j: (0, 0)),
                  pl.BlockSpec(g.shape, lambda i, j: (0, 0)),
                  pl.BlockSpec((d, tf), lambda i, j: (0, j)),
                  pl.BlockSpec((tf, d), lambda i, j: (j, 0)),
                  pl.BlockSpec(gf.shape, lambda i, j: (0, 0))],
        out_specs=pl.BlockSpec((tm, d), lambda i, j: (i, 0)),
        out_shape=jax.ShapeDtypeStruct((r, d), F32),
        scratch_shapes=[pltpu.VMEM((tm, d), F32), pltpu.VMEM((tm, d), BF16), pltpu.VMEM((tm, d), F32)],
        compiler_params=_cparams(("parallel", "arbitrary")), name="post_final" if final else "post",
    )(x, a1, a2, wo, g, w1, w2, gf)


NQ_C = H_C * DH_C
NK_C = KVH_C * DH_C


def _odd_proj_kernel(x_ref, g_ref, w_ref, k_o, v_o, q_b, k2_b, v2_b):
    hb = _rms(x_ref[...], g_ref[...]).astype(BF16)
    o = 0
    q_b[...] = (_dot(hb, w_ref[:, o:o + NQ_C]) * QSCALE_C).astype(BF16)
    o += NQ_C
    _store_row_groups(k_o, _dot(hb, w_ref[:, o:o + NK_C]), DH_C)
    o += NK_C
    _store_row_groups(v_o, _dot(hb, w_ref[:, o:o + NK_C]), DH_C)
    o += NK_C
    k2_b[...] = _dot(hb, w_ref[:, o:o + 2 * NK_C]).astype(BF16)
    o += 2 * NK_C
    v2_b[...] = _dot(hb, w_ref[:, o:o + 2 * NK_C]).astype(BF16)


def _odd_proj(x, g, w, tm):
    r, d = x.shape
    row = lambda n: pl.BlockSpec((tm, n), lambda i: (i, 0))
    full = lambda a: pl.BlockSpec(a.shape, lambda i: (0,) * a.ndim)
    return pl.pallas_call(
        _odd_proj_kernel, grid=(r // tm,),
        in_specs=[row(d), full(g), full(w)],
        out_specs=(pl.BlockSpec((tm * KVH_C, DH_C), lambda i: (i, 0)),
                   pl.BlockSpec((tm * KVH_C, DH_C), lambda i: (i, 0)),
                   row(NQ_C), row(2 * NK_C), row(2 * NK_C)),
        out_shape=(jax.ShapeDtypeStruct((r * KVH_C, DH_C), F32), jax.ShapeDtypeStruct((r * KVH_C, DH_C), F32),
                   jax.ShapeDtypeStruct((r, NQ_C), BF16), jax.ShapeDtypeStruct((r, 2 * NK_C), BF16),
                   jax.ShapeDtypeStruct((r, 2 * NK_C), BF16)),
        compiler_params=_cparams(("parallel",)), name="odd_proj",
    )(x, g, w)


SB_DONE = -150.0


def _sb_live(c_ref):
    return jnp.max(c_ref[...]) > SB_DONE


def _sb_step(z, vb, u, mask, c_ref, acc_ref, v_feature_major=False):
    tk = z.shape[1]
    ls = jnp.minimum(z, 0.0) - jnp.log2(1.0 + jnp.exp2(-jnp.abs(z)))
    lneg = ls - z
    if mask is not None:
        lneg = jnp.where(mask, lneg, 0.0)
    hi = lneg.astype(BF16)
    lo = (lneg - hi.astype(F32)).astype(BF16)
    suf = _dot(hi, u) + _dot(lo, u)
    c = c_ref[...]
    a = jnp.exp2(ls + suf + _rep(c, tk))
    if mask is not None:
        a = jnp.where(mask, a, 0.0)
    ab = a.astype(BF16)
    acc_ref[...] += _dot_nt(ab, vb) if v_feature_major else _dot(ab, vb)
    c_ref[...] = c + (suf[:, 0:1] + lneg[:, 0:1])


def _sb_attn_kernel(q_ref, k_ref, v_ref, u_ref, o_ref, qq_ref, c_ref, acc_ref, *, tq):
    i = pl.program_id(2)
    lane = lax.broadcasted_iota(jnp.int32, (1, LANES), 1)
    for g in range(G_C):
        qb = q_ref[:, (g // 2) * LANES:(g // 2 + 1) * LANES]
        keep = (lane < DH_C) if g % 2 == 0 else (lane >= DH_C)
        qq_ref[g * tq:(g + 1) * tq, :] = jnp.where(keep, qb, jnp.zeros_like(qb))
    c_ref[...] = jnp.zeros(c_ref.shape, F32)
    acc_ref[...] = jnp.zeros(acc_ref.shape, F32)

    rows = lax.broadcasted_iota(jnp.int32, (G_C * tq, 1), 0) % tq

    def block(off, mask):
        z = _dot_nt(qq_ref[...], k_ref[pl.ds(off, tq), :])
        _sb_step(z, v_ref[pl.ds(off, tq), :], u_ref[...], mask, c_ref, acc_ref)

    block(pl.multiple_of(i * tq, tq), lax.broadcasted_iota(jnp.int32, (1, tq), 1) < rows)

    @pl.when(i > 0)
    def _():
        block(pl.multiple_of((i - 1) * tq, tq), None)

    def cond(carry):
        j, live = carry
        return jnp.logical_and(j >= 0, live)

    def body(carry):
        j, _ = carry
        block(pl.multiple_of(j * tq, tq), None)
        return j - 1, _sb_live(c_ref)

    lax.while_loop(cond, body, (i - 2, _sb_live(c_ref)))
    acc = acc_ref[...]
    for c2 in range(G_C // 2):
        o_ref[:, c2 * LANES:(c2 + 1) * LANES] = jnp.where(
            lane < DH_C, acc[(2 * c2) * tq:(2 * c2 + 1) * tq], acc[(2 * c2 + 1) * tq:(2 * c2 + 2) * tq]
        ).astype(o_ref.dtype)


def _tri(tk):
    j = jnp.arange(tk)
    return (j[:, None] > j[None, :]).astype(BF16)


def _sb_attn(q, k2, v2, b, t):
    tq = _tile(t, 256)
    nq = t // tq
    u = _tri(tq)
    return pl.pallas_call(
        functools.partial(_sb_attn_kernel, tq=tq),
        grid=(b, KVH_C, nq),
        in_specs=[pl.BlockSpec((tq, G_C * DH_C), lambda bb, h, i: (bb * nq + i, h)),
                  pl.BlockSpec((t, LANES), lambda bb, h, i: (bb, h)),
                  pl.BlockSpec((t, LANES), lambda bb, h, i: (bb, h)),
                  pl.BlockSpec(u.shape, lambda bb, h, i: (0, 0))],
        out_specs=pl.BlockSpec((tq, G_C * DH_C), lambda bb, h, i: (bb * nq + i, h)),
        out_shape=jax.ShapeDtypeStruct((b * t, NQ_C), BF16),
        scratch_shapes=[pltpu.VMEM((G_C * tq, LANES), BF16), pltpu.VMEM((G_C * tq, LANES), F32),
                        pltpu.VMEM((G_C * tq, LANES), F32)],
        compiler_params=_cparams(("parallel", "parallel", "arbitrary")), name="sb_attn",
    )(q, k2, v2, u)


def _dec_even_kernel(pt_ref, qd_ref, ql_ref, qr_ref, ok_ref, ov_ref, om_ref, lam_ref, gsub_ref, *rest,
                     npg, ts, lam_init):
    kp = rest[0:npg]
    vp = rest[npg:2 * npg]
    mp = rest[2 * npg:3 * npg]
    od_ref, ol_ref = rest[3 * npg:3 * npg + 2]
    md_ref, ld_ref, accd_ref, mm_ref, lm_ref, accm_ref = rest[3 * npg + 2:]
    s_id = pl.program_id(1)
    rows_h = 2 * ts

    @pl.when(s_id == 0)
    def _():
        md_ref[...] = jnp.full(md_ref.shape, NEG, F32)
        ld_ref[...] = jnp.zeros(ld_ref.shape, F32)
        accd_ref[...] = jnp.zeros(accd_ref.shape, F32)
        mm_ref[...] = jnp.full(mm_ref.shape, NEG, F32)
        lm_ref[...] = jnp.zeros(lm_ref.shape, F32)
        accm_ref[...] = jnp.zeros(accm_ref.shape, F32)

    def softmax(s, m_ref, l_ref):
        m_prev = m_ref[...]
        m_new = jnp.maximum(m_prev, jnp.max(s, axis=-1, keepdims=True))
        alpha = jnp.exp2(m_prev - m_new)
        p = jnp.exp2(s - _rep(m_new, s.shape[1]))
        l_ref[...] = alpha * l_ref[...] + jnp.sum(p, axis=-1, keepdims=True)
        m_ref[...] = m_new
        return p.astype(BF16), alpha

    def cat(xs, axis):
        return xs[0] if len(xs) == 1 else jnp.concatenate(xs, axis=axis)

    def update(ks, vs, ms, dmask, mmask):
        sd = _dot(qd_ref[...], cat([k[...].astype(BF16) for k in ks], 1))
        if dmask is not None:
            sd = jnp.where(dmask, sd, NEG)
        pd, alpha = softmax(sd, md_ref, ld_ref)
        for h in range(H_A):
            vh = cat([v[pl.ds(h, PAGE, stride=H_A), :].astype(BF16) for v in vs], 0)
            r = slice(h * rows_h, (h + 1) * rows_h)
            accd_ref[r, :] = alpha[r] * accd_ref[r, :] + _dot(pd[r], vh)
        lat = cat([m[0:KV_LORA, :].astype(BF16) for m in ms], 1)
        rope = cat([m[KV_LORA:KV_LORA + ROPE_B, :].astype(BF16) for m in ms], 1)
        sm = _dot(ql_ref[...], lat) + _dot(qr_ref[...], rope)
        if mmask is not None:
            sm = jnp.where(mmask, sm, NEG)
        pm, alpha = softmax(sm, mm_ref, lm_ref)
        accm_ref[...] = _rep(alpha, KV_LORA) * accm_ref[...] + _dot_nt(pm, lat)

    update(kp, vp, mp, None, None)

    @pl.when(s_id == pl.num_programs(1) - 1)
    def _():
        key = lax.broadcasted_iota(jnp.int32, (1, PAGE), 1)
        tok_d = lax.broadcasted_iota(jnp.int32, (H_A * rows_h, 1), 0) % ts
        tok_m = lax.broadcasted_iota(jnp.int32, (H_B * ts, 1), 0) % ts
        update([ok_ref], [ov_ref], [om_ref], key <= tok_d, key <= tok_m)
        o = accd_ref[...] / ld_ref[...]
        lam = _lam(lam_ref, lam_init)
        for h in range(H_A):
            od = o[h * rows_h:h * rows_h + ts] - lam * o[h * rows_h + ts:(h + 1) * rows_h]
            od_ref[:, h * LANES:(h + 1) * LANES] = _rms(od, gsub_ref[...]) * (1.0 - lam_init)
        ol_ref[...] = accm_ref[...] / _rep(lm_ref[...], KV_LORA)


def _dec_even(pt, qd, ql, qr, own_k, own_v, own_m, lam_vec, gsub, ck, cv, cm, layer, npg, lam_init):
    nb, n_pages = pt.shape
    ts = qd.shape[1] // (H_A * 2)
    nsteps = n_pages // npg
    ptf = pt.reshape(-1)

    def page_spec(a, p):
        return pl.BlockSpec((None, None) + a.shape[2:],
                            lambda b, s, pt_ref: (layer, pt_ref[b * n_pages + s * npg + p], 0, 0))

    per_b = lambda a: pl.BlockSpec((None,) + a.shape[1:], lambda b, s, pt_ref: (b,) + (0,) * (a.ndim - 1))
    full = lambda a: pl.BlockSpec(a.shape, lambda b, s, pt_ref: (0,) * a.ndim)
    in_specs = ([per_b(qd), per_b(ql), per_b(qr), per_b(own_k), per_b(own_v), per_b(own_m),
                 full(lam_vec), full(gsub)]
                + [page_spec(ck, p) for p in range(npg)]
                + [page_spec(cv, p) for p in range(npg)]
                + [page_spec(cm, p) for p in range(npg)])
    rd, rm = qd.shape[1], ql.shape[1]
    grid_spec = pltpu.PrefetchScalarGridSpec(
        num_scalar_prefetch=1, grid=(nb, nsteps), in_specs=in_specs,
        out_specs=(pl.BlockSpec((None, ts, N_A), lambda b, s, pt_ref: (b, 0, 0)),
                   pl.BlockSpec((None, rm, KV_LORA), lambda b, s, pt_ref: (b, 0, 0))),
        scratch_shapes=[pltpu.VMEM((rd, LANES), F32), pltpu.VMEM((rd, LANES), F32), pltpu.VMEM((rd, LANES), F32),
                        pltpu.VMEM((rm, LANES), F32), pltpu.VMEM((rm, LANES), F32), pltpu.VMEM((rm, KV_LORA), F32)])
    return pl.pallas_call(
        functools.partial(_dec_even_kernel, npg=npg, ts=ts, lam_init=lam_init),
        grid_spec=grid_spec,
        out_shape=(jax.ShapeDtypeStruct((nb, ts, N_A), F32), jax.ShapeDtypeStruct((nb, rm, KV_LORA), F32)),
        compiler_params=_cparams(("parallel", "arbitrary")), name="dec_even",
    )(ptf, qd, ql, qr, own_k, own_v, own_m, lam_vec, gsub, *([ck] * npg), *([cv] * npg), *([cm] * npg))


def _uv_kernel(ol_ref, wuv_ref, o_ref):
    o_ref[...] = _dot(ol_ref[...].astype(BF16), wuv_ref[...]).astype(o_ref.dtype)


def _uv(ol, wuv):
    _, r, _ = ol.shape
    return pl.pallas_call(
        _uv_kernel, grid=(H_B,),
        in_specs=[pl.BlockSpec((None, r, KV_LORA), lambda h: (h, 0, 0)),
                  pl.BlockSpec((None, KV_LORA, V_B), lambda h: (h, 0, 0))],
        out_specs=pl.BlockSpec((r, V_B), lambda h: (0, h)),
        out_shape=jax.ShapeDtypeStruct((r, H_B * V_B), BF16),
        compiler_params=_cparams(("parallel",)), name="mla_uv",
    )(ol, wuv)


def _dec_sb_kernel(pt_ref, q_ref, u_ref, *rest, npg, ts, first):
    prev = rest[0:2]
    kp = rest[2:2 + npg]
    vp = rest[2 + npg:2 + 2 * npg]
    c_ref, acc_ref = rest[2 + 2 * npg:]
    s_id = pl.program_id(1)

    def block(pages, mask):
        cat = lambda xs: xs[0] if len(xs) == 1 else jnp.concatenate(xs, axis=1)
        n = len(pages) * PAGE
        z = _dot(q_ref[...], cat([k[...].astype(BF16) for k, _ in pages]))
        _sb_step(z, cat([v[...].astype(BF16) for _, v in pages]), u_ref[0:n, 0:n], mask, c_ref, acc_ref,
                 v_feature_major=True)

    todo = list(zip(kp, vp))
    if first:
        c_ref[...] = jnp.zeros(c_ref.shape, F32)
        acc_ref[...] = jnp.zeros(acc_ref.shape, F32)
        key = lax.broadcasted_iota(jnp.int32, (1, 2 * PAGE), 1)
        tok = lax.broadcasted_iota(jnp.int32, (H_C * ts, 1), 0) % ts
        block([todo[0], (prev[0], prev[1])], key < tok + PAGE)
        todo = todo[1:]
    else:
        @pl.when(s_id == 0)
        def _():
            c_ref[...] = prev[0][...]
            acc_ref[...] = prev[1][...]

    @pl.when(_sb_live(c_ref))
    def _():
        for p in range(0, len(todo), 2):
            @pl.when(_sb_live(c_ref))
            def _():
                block(todo[p:p + 2][::-1], None)


def _dec_sb(pt, q, prev, ck, cv, layer, npg, hi, nsteps, first):
    nb, n_pages = pt.shape
    ts = q.shape[1] // H_C
    ptf = pt.reshape(-1)
    w = ck.shape[-2]
    assert nsteps == 1 or not first
    u = _tri(2 * PAGE)

    def page_spec(p):
        return pl.BlockSpec((None, None, w, PAGE),
                            lambda b, s, pt_ref: (layer, pt_ref[b * n_pages + hi - 1 - (s * npg + p)], 0, 0))

    per_b = lambda a: pl.BlockSpec((None,) + a.shape[1:], lambda b, s, pt_ref: (b,) + (0,) * (a.ndim - 1))
    in_specs = ([per_b(q), pl.BlockSpec(u.shape, lambda b, s, pt_ref: (0, 0)), per_b(prev[0]), per_b(prev[1])]
                + [page_spec(p) for p in range(npg)] + [page_spec(p) for p in range(npg)])
    rq = q.shape[1]
    grid_spec = pltpu.PrefetchScalarGridSpec(
        num_scalar_prefetch=1, grid=(nb, nsteps), in_specs=in_specs,
        out_specs=(pl.BlockSpec((None, rq, LANES), lambda b, s, pt_ref: (b, 0, 0)),
                   pl.BlockSpec((None, rq, w), lambda b, s, pt_ref: (b, 0, 0))))
    return pl.pallas_call(
        functools.partial(_dec_sb_kernel, npg=npg, ts=ts, first=first),
        grid_spec=grid_spec,
        out_shape=(jax.ShapeDtypeStruct((nb, rq, LANES), F32), jax.ShapeDtypeStruct((nb, rq, w), F32)),
        compiler_params=_cparams(("parallel", "arbitrary")), name="dec_sb_first" if first else "dec_sb_rest",
    )(ptf, q, u, prev[0], prev[1], *([ck] * npg), *([cv] * npg))


def _dec_sb_all(pt, q, own_k, own_v, ck, cv, layer, npg):
    n_pages = pt.shape[1]
    state = _dec_sb(pt, q, (own_k, own_v), ck, cv, layer, npg, n_pages, 1, True)
    if n_pages > npg:
        rest = lambda st: _dec_sb(pt, q, st, ck, cv, layer, npg, n_pages - npg, n_pages // npg - 1, False)
        state = lax.cond(jnp.max(state[0]) > SB_DONE, rest, lambda st: st, state)
    return state[1]


def _pad_rows(a, n):
    return jnp.pad(a, ((0, 0), (0, n - a.shape[1]), (0, 0)))


def _feature_major(a, n):
    at = jnp.swapaxes(a.astype(BF16), 1, 2)
    return jnp.pad(at, ((0, 0), (0, 0), (0, n - at.shape[2])))


def _block_diag_queries(q, groups, width):
    nb, ts, _ = q.shape
    eye = jnp.eye(groups, dtype=q.dtype)
    qg = q.reshape(nb, ts, groups, width)
    out = jnp.einsum('btgw,hg->bhtgw', qg, eye)
    return out.reshape(nb, groups * ts, groups * width)


def kernel(x_prompt, x_sample, cache_diff_k, cache_diff_v, cache_mla, cache_sb_k, cache_sb_v, page_table,
           g_mix, g_ffn, w_in_even, diff_lambda, g_diff_sub, g_mla_q, g_mla_kv, w_mla_uq, w_mla_uk, w_mla_uv,
           w_out_even, w_in_odd, w_out_odd, w_ff1, w_ff2, g_final):
    b, t, d = x_prompt.shape
    nb, ts, _ = x_sample.shape
    n_pages = page_table.shape[1]
    past = n_pages * PAGE
    depth = g_mix.shape[0]
    n_even, n_odd = (depth + 1) // 2, depth // 2
    n_pool = cache_diff_k.shape[1]
    npg = _tile(n_pages, 8)
    npg_e = _tile(n_pages, 16)

    xp = x_prompt.reshape(b * t, d)
    xs = x_sample.reshape(nb * ts, d)
    tm_p = _tile(b * t, 512)
    tm_s = _tile(nb * ts, 512)
    tab_p = _rope_tables(jnp.arange(t))
    tab_s = jnp.tile(_rope_tables(past + jnp.arange(ts)), (tm_s // ts, 1))

    ck = jnp.transpose(cache_diff_k, (0, 1, 3, 4, 5, 2)).reshape(n_even, n_pool, N_A, PAGE)
    cv = cache_diff_v.reshape(n_even, n_pool, PAGE * H_A, 2 * DH_A)
    cm = jnp.swapaxes(cache_mla, 2, 3)
    csk = jnp.transpose(cache_sb_k, (0, 1, 3, 4, 2)).reshape(n_odd, n_pool, NK_C, PAGE)
    csv = jnp.transpose(cache_sb_v, (0, 1, 3, 4, 2)).reshape(n_odd, n_pool, NK_C, PAGE)

    row2 = lambda v: v.reshape(1, -1)
    outs_p = {k: [] for k in ("dk", "dv", "ml", "sk", "sv")}
    outs_s = {k: [] for k in ("dk", "dv", "ml", "sk", "sv")}

    for l in range(depth):
        w1 = w_ff1[l].astype(BF16)
        w2 = w_ff2[l].astype(BF16)
        final = l == depth - 1
        gf = row2(g_final)
        if l % 2 == 0:
            e = l // 2
            lam_init = 0.8 - 0.6 * math.exp(-0.3 * l)
            wi = w_in_even[e]
            c0 = 3 * N_A + Q_LORA + KV_LORA
            w_all = jnp.concatenate([wi[:, :c0]] + [wi[:, c0:]] * H_B, axis=1).astype(BF16)
            uq = w_mla_uq[e].reshape(Q_LORA, H_B, NOPE_B + ROPE_B)
            wuq = jnp.concatenate([uq[:, :, :NOPE_B].reshape(Q_LORA, H_B * NOPE_B),
                                   uq[:, :, NOPE_B:].reshape(Q_LORA, H_B * ROPE_B)], axis=1).astype(BF16)
            wuk = jnp.einsum('hcn,hg->hngc', w_mla_uk[e], jnp.eye(H_B, dtype=F32)).reshape(
                H_B * NOPE_B, H_B * KV_LORA).astype(BF16)
            wuv = w_mla_uv[e].astype(BF16)
            wo = w_out_even[e].astype(BF16)
            gq, gkv, gsub = row2(g_mla_q[e]), row2(g_mla_kv[e]), row2(g_diff_sub[e])
            lam_vec = diff_lambda[e]
            gm = row2(g_mix[l])

            kd, vd, ml, qd_b, kd_b, vd_b, qm_b, km_b = _even_proj(xp, gm, w_all, tab_p, gq, gkv, wuq, wuk, tm_p)
            outs_p["dk"].append(kd.reshape(b, t, H_A, 2, DH_A))
            outs_p["dv"].append(vd.reshape(b, t, H_A, 2 * DH_A))
            outs_p["ml"].append(ml.reshape(b, t, KV_LORA + ROPE_B))
            od = _diff_attn(qd_b, kd_b, vd_b, lam_vec, gsub, b, t, lam_init)
            om = _mla_attn(qm_b, km_b, wuv, b, t)
            xp = _post(xp, od, 0, om, 0, wo, row2(g_ffn[l]), w1, w2, gf, final)

            kd, vd, ml, qd_b, _, _, qm_b, _ = _even_proj(xs, gm, w_all, tab_s, gq, gkv, wuq, wuk, tm_s)
            outs_s["dk"].append(kd.reshape(nb, ts, H_A, 2, DH_A))
            outs_s["dv"].append(vd.reshape(nb, ts, H_A, 2 * DH_A))
            outs_s["ml"].append(ml.reshape(nb, ts, KV_LORA + ROPE_B))
            qbd = _block_diag_queries(qd_b.reshape(nb, ts, N_A), H_A * 2, DH_A)
            qm4 = qm_b.reshape(H_B, nb, ts, MLA_W)
            ql = jnp.transpose(qm4[..., :KV_LORA], (1, 0, 2, 3)).reshape(nb, H_B * ts, KV_LORA)
            qr = jnp.stack([qm4[h, :, :, KV_LORA + h * ROPE_B:KV_LORA + (h + 1) * ROPE_B] for h in range(H_B)],
                           axis=1).reshape(nb, H_B * ts, ROPE_B)
            od_s, ol_s = _dec_even(page_table, qbd, ql, qr,
                                   _feature_major(kd.reshape(nb, ts, N_A), PAGE),
                                   _pad_rows(vd.reshape(nb, ts * H_A, 2 * DH_A), PAGE * H_A),
                                   _feature_major(ml.reshape(nb, ts, KV_LORA + ROPE_B), PAGE),
                                   lam_vec, gsub, ck, cv, cm, e, npg_e, lam_init)
            ol_h = jnp.transpose(ol_s.reshape(nb, H_B, ts, KV_LORA), (1, 0, 2, 3)).reshape(H_B, nb * ts, KV_LORA)
            om_s = _uv(ol_h, wuv)
            xs = _post(xs, od_s.reshape(nb * ts, N_A), 0, om_s, 0, wo, row2(g_ffn[l]), w1, w2, gf, final)
        else:
            o = l // 2
            wi = w_in_odd[o]
            wq, wk, wv = wi[:, :NQ_C], wi[:, NQ_C:NQ_C + NK_C], wi[:, NQ_C + NK_C:]
            dup = lambda w: jnp.repeat(w.reshape(d, KVH_C, 1, DH_C), 2, axis=2).reshape(d, 2 * NK_C)
            w_all = jnp.concatenate([wq, wk, wv, dup(wk), dup(wv)], axis=1).astype(BF16)
            wo = w_out_odd[o].astype(BF16)
            gm = row2(g_mix[l])

            k, v, q_b, k2_b, v2_b = _odd_proj(xp, gm, w_all, tm_p)
            outs_p["sk"].append(k.reshape(b, t, KVH_C, DH_C))
            outs_p["sv"].append(v.reshape(b, t, KVH_C, DH_C))
            att = _sb_attn(q_b, k2_b, v2_b, b, t)
            xp = _post(xp, att, 0, att, 1, wo, row2(g_ffn[l]), w1, w2, gf, final)

            k, v, q_b, _, _ = _odd_proj(xs, gm, w_all, tm_s)
            outs_s["sk"].append(k.reshape(nb, ts, KVH_C, DH_C))
            outs_s["sv"].append(v.reshape(nb, ts, KVH_C, DH_C))
            qg = jnp.transpose(q_b.reshape(nb, ts, KVH_C, G_C, DH_C), (0, 2, 3, 1, 4))
            qbd = jnp.einsum('bkgtd,kj->bkgtjd', qg, jnp.eye(KVH_C, dtype=BF16)).reshape(
                nb, H_C * ts, NK_C)
            acc_s = _dec_sb_all(page_table, qbd, _feature_major(k.reshape(nb, ts, NK_C), PAGE),
                                _feature_major(v.reshape(nb, ts, NK_C), PAGE), csk, csv, o, npg)
            acc5 = acc_s.reshape(nb, KVH_C, G_C, ts, KVH_C, DH_C)
            att_s = jnp.stack([acc5[:, kh, :, :, kh, :] for kh in range(KVH_C)], axis=1)
            att_s = jnp.transpose(att_s, (0, 3, 1, 2, 4)).reshape(nb * ts, NQ_C)
            xs = _post(xs, att_s, 0, att_s, 1, wo, row2(g_ffn[l]), w1, w2, gf, final)

    st = lambda xs_: jnp.stack(xs_)
    return (xp.reshape(b, t, d), xs.reshape(nb, ts, d),
            st(outs_p["dk"]), st(outs_p["dv"]), st(outs_p["ml"]), st(outs_p["sk"]), st(outs_p["sv"]),
            st(outs_s["dk"]), st(outs_s["dv"]), st(outs_s["ml"]), st(outs_s["sk"]), st(outs_s["sv"]))
```
